```python
import jax, jax.numpy as jnp
from jax import lax
import numpy as np

D_MODEL = 1024
BATCH = 16
SEQ = 2048
DEPTH = 2
DEC_BATCH = 32
DEC_SEQ = 8
PAST_LEN = 16384
PAGE_SIZE = 128

HEAD_DIM = 64
D_SB = D_MODEL // 2
SB_HEADS = D_SB // HEAD_DIM
D_RW = D_MODEL - D_SB
RW_HEADS = D_RW // HEAD_DIM
LORA_W = 64
LORA_A = 64
LORA_G = 128
P_RW = 3 * D_RW + LORA_W + LORA_A + LORA_G
P_AB = 3 * D_SB + P_RW
D_CONV = D_MODEL
CONV_W = 3
D_FF = -(-8 * D_MODEL // (3 * 256)) * 256
Q_BLOCK = 128
N_AB_LAYERS = (DEPTH + 1) // 2
N_CONV_LAYERS = DEPTH // 2
RMS_EPS = 1e-6
GN_EPS = 64e-5
SB_BIAS_INIT = -8.0

kernel_name = 'stickbreak_rwkv7_shortconv_hybrid_step'


def rmsnorm(x, g):
    xf = x.astype(jnp.float32)
    return (xf * lax.rsqrt(jnp.mean(xf * xf, -1, keepdims=True) + RMS_EPS) * g).astype(x.dtype)


def heads(t, n):
    return t.reshape(t.shape[:-1] + (n, HEAD_DIM))


def split_ab(p, qg, kg):
    q = rmsnorm(heads(p[..., :D_SB], SB_HEADS), qg)
    k = rmsnorm(heads(p[..., D_SB:2 * D_SB], SB_HEADS), kg)
    v = heads(p[..., 2 * D_SB:3 * D_SB], SB_HEADS)
    return q, k, v, p[..., 3 * D_SB:]


def sb_weights(z, mask):
    z = z.astype(jnp.float32)
    log_keep = jnp.where(mask, jax.nn.log_sigmoid(-z), 0.0)
    after = lax.cumsum(log_keep, axis=z.ndim - 1, reverse=True) - log_keep
    return jnp.where(mask, jnp.exp(jax.nn.log_sigmoid(z) + after), 0.0)


def sb_attend_prompt(q, k, v, bias):
    b, s = q.shape[0], q.shape[1]
    nblk = s // Q_BLOCK
    qb = q.reshape(b, nblk, Q_BLOCK, SB_HEADS, HEAD_DIM).transpose(1, 0, 2, 3, 4)
    posb = jnp.arange(s).reshape(nblk, Q_BLOCK)
    kpos = jnp.arange(s)
    scale = HEAD_DIM ** -0.5
    bias_h = bias.astype(jnp.float32)[None, :, None, None]

    def block(args):
        qi, pi = args
        z = jnp.einsum('bqhd,bkhd->bhqk', qi, k).astype(jnp.float32) * scale + bias_h
        w = sb_weights(z, kpos[None, :] < pi[:, None])
        return jnp.einsum('bhqk,bkhd->bqhd', w.astype(v.dtype), v)

    o = lax.map(block, (qb, posb))
    return o.transpose(1, 0, 2, 3, 4).reshape(b, s, D_SB)


def sb_attend_sample(q, k_new, v_new, k_past, v_past, bias):
    b, t = q.shape[0], q.shape[1]
    p = k_past.shape[1]
    scale = HEAD_DIM ** -0.5
    z = jnp.concatenate([jnp.einsum('bqhd,bkhd->bhqk', q, k_past),
                         jnp.einsum('bqhd,bkhd->bhqk', q, k_new)], -1).astype(jnp.float32) * scale
    z = z + bias.astype(jnp.float32)[None, :, None, None]
    ti = jnp.arange(t)
    mask = jnp.concatenate([jnp.ones((t, p), bool), ti[None, :] < ti[:, None]], -1)
    w = sb_weights(z, mask).astype(v_new.dtype)
    o = jnp.einsum('bhqk,bkhd->bqhd', w[..., :p], v_past) + jnp.einsum('bhqk,bkhd->bqhd', w[..., p:], v_new)
    return o.reshape(b, t, D_SB)


def rwkv_mix(p, shift_prev, s0, mu, w0, w2, a0, a2, g2, k_k, k_a, r_k, lnx_w, lnx_b):
    b, t = p.shape[0], p.shape[1]
    f32 = jnp.float32
    prev = jnp.concatenate([shift_prev[:, None].astype(p.dtype), p[:, :-1]], 1)
    xm = (p + (prev - p) * mu).astype(f32)
    o1, o2, o3 = D_RW, 2 * D_RW, 3 * D_RW
    r, k, v = xm[..., :o1], xm[..., o1:o2], xm[..., o2:o3]
    pw = xm[..., o3:o3 + LORA_W]
    pa = xm[..., o3 + LORA_W:o3 + LORA_W + LORA_A]
    pg = xm[..., o3 + LORA_W + LORA_A:]
    log_w = -jax.nn.softplus(-(w0 + jnp.tanh(pw) @ w2)) - 0.5
    decay = jnp.exp(-jnp.exp(log_w))
    a = jax.nn.sigmoid(a0 + pa @ a2)
    g = jax.nn.sigmoid(pg) @ g2
    kk = heads(k * k_k, RW_HEADS)
    kk = kk * lax.rsqrt(jnp.maximum(jnp.sum(kk * kk, -1, keepdims=True), 1e-24))
    k = k * (1.0 + (a - 1.0) * k_a)
    rh, kh, vh = heads(r, RW_HEADS), heads(k, RW_HEADS), heads(v, RW_HEADS)
    dh, ah = heads(decay, RW_HEADS), heads(a, RW_HEADS)

    def step(s, inp):
        r_t, w_t, k_t, v_t, kk_t, a_t = inp
        s_kk = jnp.einsum('bhij,bhj->bhi', s, kk_t)
        s = (s * w_t[..., None, :] - s_kk[..., :, None] * (kk_t * a_t)[..., None, :]
             + v_t[..., :, None] * k_t[..., None, :])
        return s, jnp.einsum('bhij,bhj->bhi', s, r_t)

    seqs = tuple(jnp.moveaxis(z, 1, 0) for z in (rh, dh, kh, vh, kk, ah))
    s_t, y = lax.scan(step, s0.astype(f32), seqs)
    y = jnp.moveaxis(y, 0, 1)
    mean = jnp.mean(y, -1, keepdims=True)
    var = jnp.mean(jnp.square(y - mean), -1, keepdims=True)
    y = ((y - mean) * lax.rsqrt(var + GN_EPS)).reshape(b, t, D_RW) * lnx_w + lnx_b
    bonus = (jnp.sum(rh * kh * r_k, -1, keepdims=True) * vh).reshape(b, t, D_RW)
    out = (y + bonus) * g
    return out.astype(p.dtype), p[:, -1], s_t


def conv_mix(h, buf, w_in, conv_w, w_out):
    t = h.shape[1]
    pr = h @ w_in
    bg, cg, hv = pr[..., :D_CONV], pr[..., D_CONV:2 * D_CONV], pr[..., 2 * D_CONV:]
    u = cg * hv
    full = jnp.concatenate([buf.astype(u.dtype), u], 1)
    y = full[:, 0:t] * conv_w[0]
    for i in range(1, CONV_W):
        y = y + full[:, i:i + t] * conv_w[i]
    return (bg * y) @ w_out, full[:, t:]


def swiglu(h, wg, wu, wd):
    return (jax.nn.silu(h @ wg) * (h @ wu)) @ wd


def setup_inputs(seed: int = 0) -> dict:
    key = jax.random.key(seed)
    ks = jax.random.split(key, 40)
    f32 = jnp.float32
    nrm = lambda i, shape, sc: jax.random.normal(ks[i], shape, f32) * sc
    n_pages = PAST_LEN // PAGE_SIZE
    n_used = DEC_BATCH * n_pages
    n_phys = n_used + n_used // 4
    page_table = jax.random.permutation(ks[3], n_phys)[:n_used].reshape(DEC_BATCH, n_pages).astype(jnp.int32)
    return {
        'x_prompt': nrm(0, (BATCH, SEQ, D_MODEL), 1.0),
        'x_sample': nrm(1, (DEC_BATCH, DEC_SEQ, D_MODEL), 1.0),
        'cache_k': nrm(2, (N_AB_LAYERS, n_phys, PAGE_SIZE, SB_HEADS, HEAD_DIM), 1.0),
        'cache_v': nrm(4, (N_AB_LAYERS, n_phys, PAGE_SIZE, SB_HEADS, HEAD_DIM), 1.0),
        'state_wkv': nrm(5, (N_AB_LAYERS, DEC_BATCH, RW_HEADS, HEAD_DIM, HEAD_DIM), 0.3),
        'state_shift': nrm(6, (N_AB_LAYERS, DEC_BATCH, P_RW), 1.0),
        'state_conv': nrm(7, (N_CONV_LAYERS, DEC_BATCH, CONV_W - 1, D_CONV), 1.0),
        'page_table': page_table,
        'norm_mix': 1.0 + nrm(8, (DEPTH, D_MODEL), 0.02),
        'norm_ffn': 1.0 + nrm(9, (DEPTH, D_MODEL), 0.02),
        'w_in_ab': nrm(10, (N_AB_LAYERS, D_MODEL, P_AB), D_MODEL ** -0.5),
        'q_norm': 1.0 + nrm(11, (N_AB_LAYERS, HEAD_DIM), 0.02),
        'k_norm': 1.0 + nrm(12, (N_AB_LAYERS, HEAD_DIM), 0.02),
        'sb_bias': SB_BIAS_INIT + nrm(31, (N_AB_LAYERS, SB_HEADS), 0.1),
        'mu_rw': jax.random.uniform(ks[13], (N_AB_LAYERS, P_RW), f32),
        'w0': nrm(14, (N_AB_LAYERS, D_RW), 0.5),
        'w2': nrm(15, (N_AB_LAYERS, LORA_W, D_RW), LORA_W ** -0.5),
        'a0': nrm(16, (N_AB_LAYERS, D_RW), 0.5),
        'a2': nrm(17, (N_AB_LAYERS, LORA_A, D_RW), LORA_A ** -0.5),
        'g2': nrm(18, (N_AB_LAYERS, LORA_G, D_RW), LORA_G ** -0.5),
        'k_k': 0.85 + nrm(19, (N_AB_LAYERS, D_RW), 0.05),
        'k_a': 1.0 + nrm(20, (N_AB_LAYERS, D_RW), 0.05),
        'r_k': nrm(21, (N_AB_LAYERS, RW_HEADS, HEAD_DIM), 0.1),
        'lnx_w': 1.0 + nrm(22, (N_AB_LAYERS, D_RW), 0.02),
        'lnx_b': nrm(23, (N_AB_LAYERS, D_RW), 0.02),
        'w_out_ab': nrm(24, (N_AB_LAYERS, D_SB + D_RW, D_MODEL), (D_SB + D_RW) ** -0.5),
        'w_in_c': nrm(25, (N_CONV_LAYERS, D_MODEL, 3 * D_CONV), D_MODEL ** -0.5),
        'conv_w': nrm(26, (N_CONV_LAYERS, CONV_W, D_CONV), CONV_W ** -0.5),
        'w_out_c': nrm(27, (N_CONV_LAYERS, D_CONV, D_MODEL), D_CONV ** -0.5),
        'w_gate': nrm(28, (DEPTH, D_MODEL, D_FF), D_MODEL ** -0.5),
        'w_up': nrm(29, (DEPTH, D_MODEL, D_FF), D_MODEL ** -0.5),
        'w_down': nrm(30, (DEPTH, D_FF, D_MODEL), D_FF ** -0.5),
    }


def reference(x_prompt, x_sample, cache_k, cache_v, state_wkv, state_shift, state_conv, page_table,
              norm_mix, norm_ffn, w_in_ab, q_norm, k_norm, sb_bias, mu_rw, w0, w2, a0, a2, g2, k_k, k_a, r_k,
              lnx_w, lnx_b, w_out_ab, w_in_c, conv_w, w_out_c, w_gate, w_up, w_down):
    bp = x_prompt.shape[0]
    bs = x_sample.shape[0]
    past = page_table.shape[1] * PAGE_SIZE
    xp, xs = x_prompt, x_sample
    kp_l, vp_l, ks_l, vs_l = [], [], [], []
    wp_l, ws_l, sp_l, ss_l = [], [], [], []
    cp_l, cs_l = [], []
    for layer in range(DEPTH):
        j = layer // 2
        hp = rmsnorm(xp, norm_mix[layer])
        hs = rmsnorm(xs, norm_mix[layer])
        if layer % 2 == 0:
            rw = (mu_rw[j], w0[j], w2[j], a0[j], a2[j], g2[j], k_k[j], k_a[j], r_k[j], lnx_w[j], lnx_b[j])
            q, k, v, prw = split_ab(hp @ w_in_ab[j], q_norm[j], k_norm[j])
            o_sb = sb_attend_prompt(q, k, v, sb_bias[j])
            o_rw, sh, wkv = rwkv_mix(prw, jnp.zeros((bp, P_RW), prw.dtype),
                                     jnp.zeros((bp, RW_HEADS, HEAD_DIM, HEAD_DIM), jnp.float32), *rw)
            xp = xp + jnp.concatenate([o_sb, o_rw], -1) @ w_out_ab[j]
            kp_l.append(k)
            vp_l.append(v)
            wp_l.append(wkv)
            sp_l.append(sh)
            q, k, v, prw = split_ab(hs @ w_in_ab[j], q_norm[j], k_norm[j])
            k_past = cache_k[j][page_table].reshape(bs, past, SB_HEADS, HEAD_DIM)
            v_past = cache_v[j][page_table].reshape(bs, past, SB_HEADS, HEAD_DIM)
            o_sb = sb_attend_sample(q, k, v, k_past, v_past, sb_bias[j])
            o_rw, sh, wkv = rwkv_mix(prw, state_shift[j], state_wkv[j], *rw)
            xs = xs + jnp.concatenate([o_sb, o_rw], -1) @ w_out_ab[j]
            ks_l.append(k)
            vs_l.append(v)
            ws_l.append(wkv)
            ss_l.append(sh)
        else:
            o, buf = conv_mix(hp, jnp.zeros((bp, CONV_W - 1, D_CONV), hp.dtype), w_in_c[j], conv_w[j], w_out_c[j])
            xp = xp + o
            cp_l.append(buf)
            o, buf = conv_mix(hs, state_conv[j], w_in_c[j], conv_w[j], w_out_c[j])
            xs = xs + o
            cs_l.append(buf)
        xp = xp + swiglu(rmsnorm(xp, norm_ffn[layer]), w_gate[layer], w_up[layer], w_down[layer])
        xs = xs + swiglu(rmsnorm(xs, norm_ffn[layer]), w_gate[layer], w_up[layer], w_down[layer])
    return (xp, xs, jnp.stack(kp_l), jnp.stack(vp_l), jnp.stack(ks_l), jnp.stack(vs_l),
            jnp.stack(wp_l), jnp.stack(ws_l), jnp.stack(sp_l), jnp.stack(ss_l),
            jnp.stack(cp_l), jnp.stack(cs_l))
```

```python
import functools

import jax
import jax.numpy as jnp
from jax import lax
from jax.experimental import pallas as pl
from jax.experimental.pallas import tpu as pltpu

HEAD_DIM = 64
LORA_W = 64
LORA_A = 64
LORA_G = 128
RMS_EPS = 1e-6
GN_EPS = 64e-5
KK_EPS = 1e-24

LANES = 128
SUBLANES = 8
TOKEN_TILE = 512
ATTN_TILE = 256
RWKV_TIME_CHUNK = 32
PAGES_PER_STEP = 8
VMEM_LIMIT = 56 * 1024 * 1024

_F32 = jnp.float32
_BF16 = jnp.bfloat16


def _params(semantics):
    return pltpu.CompilerParams(dimension_semantics=semantics, vmem_limit_bytes=VMEM_LIMIT)


def _resident():
    return pl.BlockSpec(memory_space=pltpu.VMEM)


def _dot(a, b):
    return jnp.dot(a, b, preferred_element_type=_F32)


def _dot_nt(a, b):
    return lax.dot_general(a, b, (((1,), (1,)), ((), ())), preferred_element_type=_F32)


def _softplus(x):
    return jnp.maximum(x, 0.0) + jnp.log1p(jnp.exp(-jnp.abs(x)))


def _sigmoid(x):
    return 1.0 / (1.0 + jnp.exp(-x))


def _rmsnorm_rows(x, gain):
    return x * lax.rsqrt(jnp.mean(x * x, -1, keepdims=True) + RMS_EPS) * gain


def _split_hi_lo(x):
    hi = x.astype(_BF16)
    lo = (x - hi.astype(_F32)).astype(_BF16)
    return hi, lo


def _token_tile(n, t_seq):
    tm = min(TOKEN_TILE, n)
    assert n % tm == 0 and (t_seq % tm == 0 or tm % t_seq == 0), (n, tm, t_seq)
    return tm


def _in_proj_kernel(x_ref, gain_ref, w_ref, qg_ref, kg_ref, seg_ref, q_ref, k_ref, v_ref, p_ref, *, d_sb):
    h = _rmsnorm_rows(x_ref[...], gain_ref[...]).astype(_BF16)
    seg = seg_ref[...]

    def head_norm(p, g):
        return p * lax.rsqrt(_dot((p * p).astype(_BF16), seg) + RMS_EPS) * g

    q = head_norm(_dot(h, w_ref[:, 0:d_sb]), qg_ref[...])
    q_ref[...] = q * (HEAD_DIM ** -0.5)
    k_ref[...] = head_norm(_dot(h, w_ref[:, d_sb:2 * d_sb]), kg_ref[...])
    v_ref[...] = _dot(h, w_ref[:, 2 * d_sb:3 * d_sb])
    p_ref[...] = _dot(h, w_ref[:, 3 * d_sb:])


def _in_proj(x, gain, w_bf, qg, kg, seg, t_seq):
    n, d = x.shape
    p_ab = w_bf.shape[1]
    d_sb = qg.shape[1]
    p_rw = p_ab - 3 * d_sb
    tm = _token_tile(n, t_seq)
    row = lambda width: pl.BlockSpec((tm, width), lambda i: (i, 0))
    return pl.pallas_call(
        functools.partial(_in_proj_kernel, d_sb=d_sb),
        out_shape=(jax.ShapeDtypeStruct((n, d_sb), _F32), jax.ShapeDtypeStruct((n, d_sb), _F32),
                   jax.ShapeDtypeStruct((n, d_sb), _F32), jax.ShapeDtypeStruct((n, p_rw), _F32)),
        grid=(n // tm,),
        in_specs=[row(d), _resident(), _resident(), _resident(), _resident(), _resident()],
        out_specs=(row(d_sb), row(d_sb), row(d_sb), row(p_rw)),
        compiler_params=_params(("parallel",)),
        name="in_proj",
    )(x, gain, w_bf, qg, kg, seg)


def _sb_block(z, mask, carry, tri):
    sp = _softplus(z)
    log_keep = -sp
    log_sig = z - sp
    if mask is not None:
        log_keep = jnp.where(mask, log_keep, 0.0)
    hi, lo = _split_hi_lo(log_keep)
    after = _dot(hi, tri) + _dot(lo, tri) + carry
    w = jnp.exp(log_sig + after)
    if mask is not None:
        w = jnp.where(mask, w, 0.0)
    return w, carry + jnp.sum(log_keep, -1, keepdims=True)


def _sb_prompt_kernel(bias_ref, q_ref, k_ref, v_ref, tri_ref, o_ref, *, tile):
    pair = pl.program_id(1)
    i = pl.program_id(2)
    tri = tri_ref[...]
    lane_head = lax.broadcasted_iota(jnp.int32, (tile, LANES), 1) // HEAD_DIM
    q = q_ref[0].astype(_BF16)
    qh = [jnp.where(lane_head == hh, q, jnp.zeros_like(q)) for hh in range(2)]
    bias = [bias_ref[pair * 2 + hh] for hh in range(2)]

    def visit(j, state, mask):
        kb = k_ref[0, pl.ds(pl.multiple_of(j * tile, tile), tile), :].astype(_BF16)
        vb = v_ref[0, pl.ds(pl.multiple_of(j * tile, tile), tile), :].astype(_BF16)
        out = []
        for hh in range(2):
            carry, acc = state[hh]
            w, carry = _sb_block(_dot_nt(qh[hh], kb) + bias[hh], mask, carry, tri)
            out.append((carry, acc + _dot(w.astype(_BF16), vb)))
        return tuple(out)

    rows = lax.broadcasted_iota(jnp.int32, (tile, tile), 0)
    cols = lax.broadcasted_iota(jnp.int32, (tile, tile), 1)
    zero = (jnp.zeros((tile, 1), _F32), jnp.zeros((tile, LANES), _F32))
    state = visit(i, (zero, zero), cols < rows)
    state = lax.fori_loop(0, i, lambda n, s: visit(i - 1 - n, s, None), state)
    o_ref[0] = jnp.where(lane_head == 0, state[0][1], state[1][1]).astype(o_ref.dtype)


def _sb_attend_prompt(q, k, v, bias, tri, batch, seq):
    n, d_sb = q.shape
    tile = min(ATTN_TILE, seq)
    assert seq % tile == 0 and d_sb % LANES == 0
    q3, k3, v3 = (t.reshape(batch, seq, d_sb) for t in (q, k, v))
    full = pl.BlockSpec((1, seq, LANES), lambda b, p, i: (b, 0, p))
    blk = pl.BlockSpec((1, tile, LANES), lambda b, p, i: (b, i, p))
    out = pl.pallas_call(
        functools.partial(_sb_prompt_kernel, tile=tile),
        out_shape=jax.ShapeDtypeStruct((batch, seq, d_sb), _BF16),
        grid=(batch, d_sb // LANES, seq // tile),
        in_specs=[pl.BlockSpec(memory_space=pltpu.SMEM), blk, full, full, _resident()],
        out_specs=blk,
        compiler_params=_params(("parallel", "parallel", "arbitrary")),
        name="sb_prompt",
    )(bias, q3, k3, v3, tri)
    return out.reshape(n, d_sb)


def _sb_sample_kernel(pt_ref, q_ref, kn_ref, vn_ref, bias_ref, tri_ref, *rest, t_new, n_heads, pages):
    del pt_ref
    page_refs, (o_ref, acc_ref, carry_ref) = rest[:2 * pages], rest[2 * pages:]
    g = pl.program_id(1)
    rows_q = n_heads * t_new
    d_sb = n_heads * HEAD_DIM
    tri = tri_ref[...]
    bias = bias_ref[...][:, 0:1]
    row_head = lax.broadcasted_iota(jnp.int32, (rows_q, d_sb), 0) // t_new
    lane_head = lax.broadcasted_iota(jnp.int32, (rows_q, d_sb), 1) // HEAD_DIM
    q_rows = jnp.concatenate([q_ref[0]] * n_heads, axis=0)
    qbd = jnp.where(row_head == lane_head, q_rows, 0.0).astype(_BF16)

    def visit(kb, vb, mask):
        w, carry = _sb_block(_dot_nt(qbd, kb.astype(_BF16)) + bias, mask, carry_ref[...], tri)
        carry_ref[...] = carry
        acc_ref[...] += _dot(w.astype(_BF16), vb.astype(_BF16))

    @pl.when(g == 0)
    def _():
        carry_ref[...] = jnp.zeros_like(carry_ref)
        acc_ref[...] = jnp.zeros_like(acc_ref)
        pad = jnp.zeros((LANES - t_new, d_sb), _F32)
        t_row = lax.broadcasted_iota(jnp.int32, (rows_q, LANES), 0) % t_new
        key = lax.broadcasted_iota(jnp.int32, (rows_q, LANES), 1)
        visit(jnp.concatenate([kn_ref[0], pad], 0), jnp.concatenate([vn_ref[0], pad], 0), key < t_row)

    for r in range(pages):
        visit(page_refs[r][0], page_refs[pages + r][0], None)

    @pl.when(g == pl.num_programs(1) - 1)
    def _():
        acc = acc_ref[...]
        lane_h = lax.broadcasted_iota(jnp.int32, (t_new, d_sb), 1) // HEAD_DIM
        out = jnp.zeros((t_new, d_sb), _F32)
        for hh in range(n_heads):
            out = jnp.where(lane_h == hh, acc[hh * t_new:(hh + 1) * t_new, :], out)
        o_ref[0] = out


def _sb_attend_sample(q, k_new, v_new, cache_k, cache_v, page_table, bias_rows, tri, batch, t_new):
    n, d_sb = q.shape
    n_heads = d_sb // HEAD_DIM
    n_pages = page_table.shape[1]
    page = cache_k.shape[1]
    pages = min(PAGES_PER_STEP, n_pages)
    assert page == LANES and n_pages % pages == 0 and t_new == SUBLANES
    groups = n_pages // pages
    q3, k3, v3 = (t.reshape(batch, t_new, d_sb) for t in (q, k_new, v_new))
    ck = cache_k.reshape(cache_k.shape[0], page, d_sb)
    cv = cache_v.reshape(cache_v.shape[0], page, d_sb)
    tok = pl.BlockSpec((1, t_new, d_sb), lambda b, g, pt: (b, 0, 0))

    def page_spec(r):
        return pl.BlockSpec((1, page, d_sb),
                            lambda b, g, pt: (pt[b, (groups - 1 - g) * pages + (pages - 1 - r)], 0, 0))

    out = pl.pallas_call(
        functools.partial(_sb_sample_kernel, t_new=t_new, n_heads=n_heads, pages=pages),
        out_shape=jax.ShapeDtypeStruct((batch, t_new, d_sb), _F32),
        grid_spec=pltpu.PrefetchScalarGridSpec(
            num_scalar_prefetch=1,
            grid=(batch, groups),
            in_specs=[tok, tok, tok,
                      pl.BlockSpec(bias_rows.shape, lambda b, g, pt: (0, 0)),
                      pl.BlockSpec(tri.shape, lambda b, g, pt: (0, 0))]
                     + [page_spec(r) for r in range(pages)] + [page_spec(r) for r in range(pages)],
            out_specs=tok,
            scratch_shapes=[pltpu.VMEM((n_heads * t_new, d_sb), _F32), pltpu.VMEM((n_heads * t_new, 1), _F32)],
        ),
        compiler_params=_params(("parallel", "arbitrary")),
        name="sb_sample",
    )(page_table, q3, k3, v3, bias_rows, tri, *([ck] * pages), *([cv] * pages))
    return out.reshape(n, d_sb)


def _seq_layout(tm, t_seq):
    return (tm // t_seq, t_seq) if tm >= t_seq else (1, tm)


def _boundary_rows(init_ref, carry_ref, tm, t_seq, depth):
    nseq, rows = _seq_layout(tm, t_seq)
    if nseq == 1:
        tiles_per_seq = t_seq // tm

        @pl.when(pl.program_id(0) % tiles_per_seq == 0)
        def _():
            carry_ref[...] = init_ref[0]

        c = carry_ref[...]
        return [jnp.broadcast_to(c[r:r + 1, :], (tm, c.shape[1])) for r in range(depth)]
    c = init_ref[...]
    d = c.shape[2]
    return [jnp.broadcast_to(c[:, r:r + 1, :], (nseq, rows, d)).reshape(tm, d) for r in range(depth)]


def _shifted(x, boundary, tm, t_seq, shift, depth):
    _, rows = _seq_layout(tm, t_seq)
    pos = lax.broadcasted_iota(jnp.int32, x.shape, 0) % rows
    out = pltpu.roll(x, shift, 0)
    for s in range(shift):
        out = jnp.where(pos == s, boundary[depth - (shift - s)], out)
    return out


def _init_spec(tm, t_seq, depth, d):
    nseq, _ = _seq_layout(tm, t_seq)
    if nseq == 1:
        tiles_per_seq = t_seq // tm
        return pl.BlockSpec((1, depth, d), lambda i: (i // tiles_per_seq, 0, 0))
    return pl.BlockSpec((nseq, depth, d), lambda i: (i, 0, 0))


def _rwkv_prep_kernel(p_ref, init_ref, mu_ref, w0_ref, a0_ref, w2_ref, a2_ref, g2_ref,
                      r_ref, k_ref, v_ref, w_ref, a_ref, g_ref, carry_ref, *, tm, t_seq, d_rw):
    p = p_ref[...]
    boundary = _boundary_rows(init_ref, carry_ref, tm, t_seq, 1)
    prev = _shifted(p, boundary, tm, t_seq, 1, 1)
    if _seq_layout(tm, t_seq)[0] == 1:
        carry_ref[...] = p[tm - 1:tm, :]
    xm = p + (prev - p) * mu_ref[...]
    r_ref[...] = xm[:, 0:d_rw]
    k_ref[...] = xm[:, d_rw:2 * d_rw]
    v_ref[...] = xm[:, 2 * d_rw:3 * d_rw]
    lora_wa = xm[:, 3 * d_rw:3 * d_rw + LORA_W + LORA_A]
    pg = xm[:, 3 * d_rw + LORA_W + LORA_A:]
    lw = _dot(jnp.tanh(lora_wa).astype(_BF16), w2_ref[...])
    la = _dot(lora_wa.astype(_BF16), a2_ref[...])
    log_w = -_softplus(-(w0_ref[...] + lw)) - 0.5
    w_ref[...] = jnp.exp(-jnp.exp(log_w))
    a_ref[...] = _sigmoid(a0_ref[...] + la)
    g_ref[...] = _dot(_sigmoid(pg).astype(_BF16), g2_ref[...])


def _rwkv_prep(prw, shift_init, mu, w0, a0, w2p, a2p, g2, t_seq):
    n, p_rw = prw.shape
    d_rw = w0.shape[1]
    tm = _token_tile(n, t_seq)
    row = lambda width: pl.BlockSpec((tm, width), lambda i: (i, 0))
    outs = tuple(jax.ShapeDtypeStruct((n, d_rw), _F32) for _ in range(6))
    return pl.pallas_call(
        functools.partial(_rwkv_prep_kernel, tm=tm, t_seq=t_seq, d_rw=d_rw),
        out_shape=outs,
        grid=(n // tm,),
        in_specs=[row(p_rw), _init_spec(tm, t_seq, 1, p_rw)] + [_resident()] * 6,
        out_specs=tuple(row(d_rw) for _ in range(6)),
        scratch_shapes=[pltpu.VMEM((1, p_rw), _F32)],
        compiler_params=_params(("arbitrary",)),
        name="rwkv_prep",
    )(prw, shift_init, mu, w0, a0, w2p, a2p, g2)


def _rwkv_scan_kernel(r_ref, k_ref, v_ref, w_ref, a_ref, kk_tab, ka_tab, rk_tab, lnw_tab, lnb_tab, s0_ref,
                      y_ref, s_out_ref, s_ref, kk_ref, b_ref, k2_ref, *, steps):
    c = pl.program_id(1)
    n = HEAD_DIM

    @pl.when(c == 0)
    def _():
        s_ref[...] = s0_ref[...]

    def col_sum(x):
        return jnp.sum(x, axis=0, keepdims=True)

    def step(t, _):
        r_t, k_t, v_t, a_t = r_ref[t], k_ref[t], v_ref[t], a_ref[t]
        kk = k_t * kk_tab[...]
        kk = kk * lax.rsqrt(jnp.maximum(col_sum(kk * kk), KK_EPS))
        k2 = k_t * (1.0 + (a_t - 1.0) * ka_tab[...])
        kk_ref[...] = kk
        b_ref[...] = kk * a_t
        k2_ref[...] = k2
        bonus = col_sum(r_t * k2 * rk_tab[...]) * v_t

        def dot_kk(j, acc):
            return acc + s_ref[j] * kk_ref[pl.ds(j, 1), :]

        s_kk = lax.fori_loop(0, n, dot_kk, jnp.zeros((n, LANES), _F32), unroll=8)

        def update(j, acc):
            row = pl.ds(j, 1)
            s_j = s_ref[j] * w_ref[t, row, :] - s_kk * b_ref[row, :] + v_t * k2_ref[row, :]
            s_ref[j] = s_j
            return acc + s_j * r_ref[t, row, :]

        y = lax.fori_loop(0, n, update, jnp.zeros((n, LANES), _F32), unroll=8)
        mean = col_sum(y) * (1.0 / n)
        yc = y - mean
        var = col_sum(yc * yc) * (1.0 / n)
        y_ref[t] = yc * lax.rsqrt(var + GN_EPS) * lnw_tab[...] + lnb_tab[...] + bonus
        return 0

    lax.fori_loop(0, steps, step, 0)

    @pl.when(c == pl.num_programs(1) - 1)
    def _():
        s_out_ref[...] = s_ref[...]


def _rwkv_scan(r, k, v, w, a, tabs, s0):
    t_len, n, lanes = r.shape
    steps = min(RWKV_TIME_CHUNK, t_len)
    assert t_len % steps == 0 and lanes % LANES == 0 and n == HEAD_DIM
    seq = pl.BlockSpec((steps, n, LANES), lambda g, c: (c, 0, g))
    tab = pl.BlockSpec((n, LANES), lambda g, c: (0, g))
    st = pl.BlockSpec((n, n, LANES), lambda g, c: (0, 0, g))
    return pl.pallas_call(
        functools.partial(_rwkv_scan_kernel, steps=steps),
        out_shape=(jax.ShapeDtypeStruct((t_len, n, lanes), _F32), jax.ShapeDtypeStruct((n, n, lanes), _F32)),
        grid=(lanes // LANES, t_len // steps),
        in_specs=[seq] * 5 + [tab] * 5 + [st],
        out_specs=(seq, st),
        scratch_shapes=[pltpu.VMEM((n, n, LANES), _F32)] + [pltpu.VMEM((n, LANES), _F32)] * 3,
        compiler_params=_params(("parallel", "arbitrary")),
        name="rwkv_scan",
    )(r, k, v, w, a, *tabs, s0)


def _to_lanes(x, batch, t_len, lanes):
    h = x.shape[1] // HEAD_DIM
    y = x.reshape(batch, t_len, h, HEAD_DIM).transpose(1, 3, 0, 2).reshape(t_len, HEAD_DIM, batch * h)
    return jnp.pad(y, ((0, 0), (0, 0), (0, lanes - batch * h)))


def _from_lanes(y, batch, heads):
    t_len = y.shape[0]
    y = y[:, :, :batch * heads].reshape(t_len, HEAD_DIM, batch, heads)
    return y.transpose(2, 0, 3, 1).reshape(batch * t_len, heads * HEAD_DIM)


def _table(x, batch, lanes):
    h = x.size // HEAD_DIM
    t = jnp.tile(x.reshape(h, HEAD_DIM).T, (1, batch))
    return jnp.pad(t, ((0, 0), (0, lanes - batch * h)))


def _out_ffn_kernel(*refs, gated, d_half):
    if gated:
        x_ref, ma_ref, y_ref, g_ref, wo_ref, gain_ref, wg_ref, wu_ref, wd_ref, o_ref = refs
        mb = (y_ref[...] * g_ref[...]).astype(_BF16)
    else:
        x_ref, ma_ref, mb_ref, wo_ref, gain_ref, wg_ref, wu_ref, wd_ref, o_ref = refs
        mb = mb_ref[...].astype(_BF16)
    ma = ma_ref[...].astype(_BF16)
    x1 = x_ref[...] + _dot(ma, wo_ref[0:d_half, :]) + _dot(mb, wo_ref[d_half:, :])
    h = _rmsnorm_rows(x1, gain_ref[...]).astype(_BF16)
    gate = _dot(h, wg_ref[...])
    act = (gate * _sigmoid(gate) * _dot(h, wu_ref[...])).astype(_BF16)
    o_ref[...] = x1 + _dot(act, wd_ref[...])


def _out_ffn(x, mix, wo_bf, gain, wg_bf, wu_bf, wd_bf, t_seq):
    n, d = x.shape
    d_half = wo_bf.shape[0] // 2
    tm = _token_tile(n, t_seq)
    gated = len(mix) == 3
    if gated or mix[0] is not mix[1]:
        mix_specs = [pl.BlockSpec((tm, d_half), lambda i: (i, 0))] * len(mix)
    else:
        mix_specs = [pl.BlockSpec((tm, d_half), lambda i: (i, 0)), pl.BlockSpec((tm, d_half), lambda i: (i, 1))]
    return pl.pallas_call(
        functools.partial(_out_ffn_kernel, gated=gated, d_half=d_half),
        out_shape=jax.ShapeDtypeStruct((n, d), _F32),
        grid=(n // tm,),
        in_specs=[pl.BlockSpec((tm, d), lambda i: (i, 0))] + mix_specs + [_resident()] * 5,
        out_specs=pl.BlockSpec((tm, d), lambda i: (i, 0)),
        compiler_params=_params(("parallel",)),
        name="out_ffn",
    )(x, *mix, wo_bf, gain, wg_bf, wu_bf, wd_bf)


def _conv_mix_kernel(x_ref, gain_ref, w_ref, cw_ref, init_ref, m_ref, state_ref, carry_ref, *, tm, t_seq, d_conv):
    h = _rmsnorm_rows(x_ref[...], gain_ref[...]).astype(_BF16)
    u = _dot(h, w_ref[:, d_conv:2 * d_conv]) * _dot(h, w_ref[:, 2 * d_conv:])
    depth = cw_ref.shape[0] - 1
    boundary = _boundary_rows(init_ref, carry_ref, tm, t_seq, depth)
    y = u * cw_ref[depth:depth + 1, :]
    for s in range(1, depth + 1):
        y = y + _shifted(u, boundary, tm, t_seq, s, depth) * cw_ref[depth - s:depth - s + 1, :]
    m_ref[...] = (_dot(h, w_ref[:, 0:d_conv]) * y).astype(m_ref.dtype)
    nseq, rows = _seq_layout(tm, t_seq)
    if nseq == 1:
        carry_ref[...] = u[tm - depth:, :]
        state_ref[0] = u[tm - depth:, :]
    else:
        state_ref[...] = u.reshape(nseq, rows, d_conv)[:, rows - depth:, :]


def _conv_mix(x, gain, w_bf, conv_w, init, t_seq):
    n, d = x.shape
    d_conv = conv_w.shape[1]
    depth = conv_w.shape[0] - 1
    tm = _token_tile(n, t_seq)
    return pl.pallas_call(
        functools.partial(_conv_mix_kernel, tm=tm, t_seq=t_seq, d_conv=d_conv),
        out_shape=(jax.ShapeDtypeStruct((n, d_conv), _BF16), jax.ShapeDtypeStruct(init.shape, _F32)),
        grid=(n // tm,),
        in_specs=[pl.BlockSpec((tm, d), lambda i: (i, 0)), _resident(), _resident(), _resident(),
                  _init_spec(tm, t_seq, depth, d_conv)],
        out_specs=(pl.BlockSpec((tm, d_conv), lambda i: (i, 0)), _init_spec(tm, t_seq, depth, d_conv)),
        scratch_shapes=[pltpu.VMEM((depth, d_conv), _F32)],
        compiler_params=_params(("arbitrary",)),
        name="conv_mix",
    )(x, gain, w_bf, conv_w, init)


def _strict_upper_sum_matrix(n):
    j = lax.broadcasted_iota(jnp.int32, (n, n), 0)
    s = lax.broadcasted_iota(jnp.int32, (n, n), 1)
    return (j > s).astype(_BF16)


def kernel(x_prompt, x_sample, cache_k, cache_v, state_wkv, state_shift, state_conv, page_table, norm_mix, norm_ffn, w_in_ab, q_norm, k_norm, sb_bias, mu_rw, w0, w2, a0, a2, g2, k_k, k_a, r_k, lnx_w, lnx_b, w_out_ab, w_in_c, conv_w, w_out_c, w_gate, w_up, w_down):
    bp, sp, d = x_prompt.shape
    bs, ts, _ = x_sample.shape
    depth = norm_mix.shape[0]
    d_rw = w0.shape[1]
    d_sb = w_out_ab.shape[1] - d_rw
    sb_heads, rw_heads = d_sb // HEAD_DIM, d_rw // HEAD_DIM
    p_rw = mu_rw.shape[1]
    groups = ((x_prompt.reshape(bp * sp, d), bp, sp), (x_sample.reshape(bs * ts, d), bs, ts))
    xs = [g[0] for g in groups]
    bf = lambda t: t.astype(_BF16)
    row = lambda t: t.reshape(1, -1)

    head_id = jnp.arange(d_sb) // HEAD_DIM
    seg = jnp.where(head_id[:, None] == head_id[None, :], 1.0 / HEAD_DIM, 0.0).astype(_BF16)
    tri_prompt = _strict_upper_sum_matrix(min(ATTN_TILE, sp))
    tri_page = _strict_upper_sum_matrix(cache_k.shape[2])

    outs = {name: ([], []) for name in ("k", "v", "wkv", "shift", "conv")}
    for layer in range(depth):
        j = layer // 2
        if layer % 2 == 0:
            w_in = bf(w_in_ab[j])
            qg, kg = row(jnp.tile(q_norm[j], sb_heads)), row(jnp.tile(k_norm[j], sb_heads))
            w2p = bf(jnp.concatenate([w2[j], jnp.zeros((LORA_A, d_rw), _F32)], 0))
            a2p = bf(jnp.concatenate([jnp.zeros((LORA_W, d_rw), _F32), a2[j]], 0))
            bias_rows = jnp.broadcast_to(jnp.repeat(sb_bias[j], ts)[:, None], (sb_heads * ts, LANES))
            for gi, (_, batch, t_len) in enumerate(groups):
                x = xs[gi]
                q, k, v, prw = _in_proj(x, row(norm_mix[layer]), w_in, qg, kg, seg, t_len)
                if gi == 0:
                    o_sb = _sb_attend_prompt(q, k, v, sb_bias[j], tri_prompt, batch, t_len)
                    shift_init = jnp.zeros((batch, 1, p_rw), _F32)
                    wkv0 = jnp.zeros((batch, rw_heads, HEAD_DIM, HEAD_DIM), _F32)
                else:
                    o_sb = _sb_attend_sample(q, k, v, cache_k[j], cache_v[j], page_table, bias_rows, tri_page,
                                             batch, t_len)
                    shift_init = state_shift[j][:, None, :]
                    wkv0 = state_wkv[j]
                r_, k_, v_, w_, a_, g_ = _rwkv_prep(prw, shift_init, row(mu_rw[j]), row(w0[j]), row(a0[j]),
                                                    w2p, a2p, bf(g2[j]), t_len)
                lanes = -(-batch * rw_heads // LANES) * LANES
                seqs = [_to_lanes(t, batch, t_len, lanes) for t in (r_, k_, v_, w_, a_)]
                tabs = [_table(t, batch, lanes) for t in (k_k[j], k_a[j], r_k[j], lnx_w[j], lnx_b[j])]
                s0 = jnp.pad(wkv0.transpose(3, 2, 0, 1).reshape(HEAD_DIM, HEAD_DIM, batch * rw_heads),
                             ((0, 0), (0, 0), (0, lanes - batch * rw_heads)))
                y_l, s_l = _rwkv_scan(*seqs, tabs, s0)
                y_rw = _from_lanes(y_l, batch, rw_heads)
                wkv = s_l[:, :, :batch * rw_heads].reshape(HEAD_DIM, HEAD_DIM, batch, rw_heads).transpose(2, 3, 1, 0)
                xs[gi] = _out_ffn(x, (o_sb, y_rw, g_), bf(w_out_ab[j]), row(norm_ffn[layer]),
                                  bf(w_gate[layer]), bf(w_up[layer]), bf(w_down[layer]), t_len)
                outs["k"][gi].append(k.reshape(batch, t_len, sb_heads, HEAD_DIM))
                outs["v"][gi].append(v.reshape(batch, t_len, sb_heads, HEAD_DIM))
                outs["wkv"][gi].append(wkv)
                outs["shift"][gi].append(prw.reshape(batch, t_len, p_rw)[:, -1])
        else:
            for gi, (_, batch, t_len) in enumerate(groups):
                x = xs[gi]
                init = jnp.zeros((batch, conv_w.shape[1] - 1, conv_w.shape[2]), _F32) if gi == 0 else state_conv[j]
                m, buf = _conv_mix(x, row(norm_mix[layer]), bf(w_in_c[j]), conv_w[j], init, t_len)
                xs[gi] = _out_ffn(x, (m, m), bf(w_out_c[j]), row(norm_ffn[layer]),
                                  bf(w_gate[layer]), bf(w_up[layer]), bf(w_down[layer]), t_len)
                outs["conv"][gi].append(buf)
    stack = lambda name, gi: jnp.stack(outs[name][gi])
    return (xs[0].reshape(bp, sp, d), xs[1].reshape(bs, ts, d),
            stack("k", 0), stack("v", 0), stack("k", 1), stack("v", 1),
            stack("wkv", 0), stack("wkv", 1), stack("shift", 0), stack("shift", 1),
            stack("conv", 0), stack("conv", 1))
```

```python
import functools

import jax
import jax.numpy as jnp
from jax import lax
from jax.experimental import pallas as pl
from jax.experimental.pallas import tpu as pltpu

HEAD_DIM = 64
LORA_W = 64
LORA_A = 64
LORA_G = 128
RMS_EPS = 1e-6
GN_EPS = 64e-5
KK_EPS = 1e-24

LANES = 128
SUBLANES = 8
TOKEN_TILE = 512
ATTN_TILE = 256
ATTN_HEADS = 4
RWKV_TIME_CHUNK = 32
PAGES_PER_STEP = 8
VMEM_LIMIT = 56 * 1024 * 1024

_F32 = jnp.float32
_BF16 = jnp.bfloat16


def _params(semantics):
    return pltpu.CompilerParams(dimension_semantics=semantics, vmem_limit_bytes=VMEM_LIMIT)


def _resident():
    return pl.BlockSpec(memory_space=pltpu.VMEM)


def _dot(a, b):
    return jnp.dot(a, b, preferred_element_type=_F32)


def _dot_nt(a, b):
    return lax.dot_general(a, b, (((1,), (1,)), ((), ())), preferred_element_type=_F32)


def _softplus(x):
    return jnp.maximum(x, 0.0) + jnp.log(1.0 + jnp.exp(-jnp.abs(x)))


def _sigmoid(x):
    return 1.0 / (1.0 + jnp.exp(-x))


def _rmsnorm_rows(x, gain):
    return x * lax.rsqrt(jnp.mean(x * x, -1, keepdims=True) + RMS_EPS) * gain


def _split_hi_lo(x):
    hi = x.astype(_BF16)
    lo = (x - hi.astype(_F32)).astype(_BF16)
    return hi, lo


def _token_tile(n, t_seq):
    tm = min(TOKEN_TILE, n)
    assert n % tm == 0 and (t_seq % tm == 0 or tm % t_seq == 0), (n, tm, t_seq)
    return tm


def _in_proj_kernel(x_ref, gain_ref, w_ref, qg_ref, kg_ref, seg_ref, q_ref, k_ref, v_ref, p_ref, *, d_sb):
    h = _rmsnorm_rows(x_ref[...], gain_ref[...]).astype(_BF16)
    seg = seg_ref[...]

    def head_norm(p, g):
        return p * lax.rsqrt(_dot((p * p).astype(_BF16), seg) + RMS_EPS) * g

    q = head_norm(_dot(h, w_ref[:, 0:d_sb]), qg_ref[...])
    q_ref[...] = q * (HEAD_DIM ** -0.5)
    k_ref[...] = head_norm(_dot(h, w_ref[:, d_sb:2 * d_sb]), kg_ref[...])
    v_ref[...] = _dot(h, w_ref[:, 2 * d_sb:3 * d_sb])
    p_ref[...] = _dot(h, w_ref[:, 3 * d_sb:])


def _in_proj(x, gain, w_bf, qg, kg, seg, t_seq):
    n, d = x.shape
    p_ab = w_bf.shape[1]
    d_sb = qg.shape[1]
    p_rw = p_ab - 3 * d_sb
    tm = _token_tile(n, t_seq)
    row = lambda width: pl.BlockSpec((tm, width), lambda i: (i, 0))
    return pl.pallas_call(
        functools.partial(_in_proj_kernel, d_sb=d_sb),
        out_shape=(jax.ShapeDtypeStruct((n, d_sb), _F32), jax.ShapeDtypeStruct((n, d_sb), _F32),
                   jax.ShapeDtypeStruct((n, d_sb), _F32), jax.ShapeDtypeStruct((n, p_rw), _F32)),
        grid=(n // tm,),
        in_specs=[row(d), _resident(), _resident(), _resident(), _resident(), _resident()],
        out_specs=(row(d_sb), row(d_sb), row(d_sb), row(p_rw)),
        compiler_params=_params(("parallel",)),
        name="in_proj",
    )(x, gain, w_bf, qg, kg, seg)


def _suffix_sums(log_keep, tri):
    m = log_keep.shape[0]
    r = _dot(jnp.concatenate(_split_hi_lo(log_keep), 0), tri)
    return r[:m] + r[m:]


def _sb_block(z, mask, carry, tri):
    sp = _softplus(z)
    log_keep = -sp
    log_sig = z - sp
    if mask is not None:
        log_keep = jnp.where(mask, log_keep, 0.0)
    after = _suffix_sums(log_keep, tri) + carry
    w = jnp.exp(log_sig + after)
    if mask is not None:
        w = jnp.where(mask, w, 0.0)
    return w, carry + jnp.sum(log_keep, -1, keepdims=True)


def _sb_prompt_kernel(bias_ref, q_ref, k_ref, v_ref, tri_ref, o_ref, *, tile, heads):
    group = pl.program_id(1)
    i = pl.program_id(2)
    tri = tri_ref[...]
    lane_head = lax.broadcasted_iota(jnp.int32, (tile, LANES), 1) // HEAD_DIM
    lanes_of = lambda hh: slice((hh // 2) * LANES, (hh // 2 + 1) * LANES)
    qh = []
    for hh in range(heads):
        q = q_ref[0, :, lanes_of(hh)].astype(_BF16)
        qh.append(jnp.where(lane_head == hh % 2, q, jnp.zeros_like(q)))
    bias = [bias_ref[group * heads + hh] for hh in range(heads)]

    def visit(j, state, mask):
        start = pl.multiple_of(j * tile, tile)
        out = []
        for hh in range(heads):
            kb = k_ref[0, pl.ds(start, tile), lanes_of(hh)].astype(_BF16)
            vb = v_ref[0, pl.ds(start, tile), lanes_of(hh)].astype(_BF16)
            carry, acc = state[hh]
            w, carry = _sb_block(_dot_nt(qh[hh], kb) + bias[hh], mask, carry, tri)
            out.append((carry, acc + _dot(w.astype(_BF16), vb)))
        return tuple(out)

    rows = lax.broadcasted_iota(jnp.int32, (tile, tile), 0)
    cols = lax.broadcasted_iota(jnp.int32, (tile, tile), 1)
    zero = (jnp.zeros((tile, 1), _F32), jnp.zeros((tile, LANES), _F32))
    state = visit(i, (zero,) * heads, cols < rows)
    state = lax.fori_loop(0, i, lambda n, s: visit(i - 1 - n, s, None), state)
    for p in range(heads // 2):
        o_ref[0, :, p * LANES:(p + 1) * LANES] = jnp.where(
            lane_head == 0, state[2 * p][1], state[2 * p + 1][1]).astype(o_ref.dtype)


def _sb_attend_prompt(q, k, v, bias, tri, batch, seq):
    n, d_sb = q.shape
    tile = min(ATTN_TILE, seq)
    width = ATTN_HEADS * HEAD_DIM
    assert seq % tile == 0 and d_sb % width == 0 and width % LANES == 0
    q3, k3, v3 = (t.reshape(batch, seq, d_sb) for t in (q, k, v))
    full = pl.BlockSpec((1, seq, width), lambda b, p, i: (b, 0, p))
    blk = pl.BlockSpec((1, tile, width), lambda b, p, i: (b, i, p))
    out = pl.pallas_call(
        functools.partial(_sb_prompt_kernel, tile=tile, heads=ATTN_HEADS),
        out_shape=jax.ShapeDtypeStruct((batch, seq, d_sb), _BF16),
        grid=(batch, d_sb // width, seq // tile),
        in_specs=[pl.BlockSpec(memory_space=pltpu.SMEM), blk, full, full, _resident()],
        out_specs=blk,
        compiler_params=_params(("parallel", "parallel", "arbitrary")),
        name="sb_prompt",
    )(bias, q3, k3, v3, tri)
    return out.reshape(n, d_sb)


def _sb_sample_kernel(pt_ref, q_ref, kn_ref, vn_ref, bias_ref, tri_ref, *rest, t_new, n_heads, pages):
    del pt_ref
    page_refs, (o_ref, acc_ref, carry_ref) = rest[:2 * pages], rest[2 * pages:]
    g = pl.program_id(1)
    rows_q = n_heads * t_new
    d_sb = n_heads * HEAD_DIM
    page = tri_ref.shape[0]
    tri = tri_ref[...]
    bias = bias_ref[...][:, 0:1]
    row_head = lax.broadcasted_iota(jnp.int32, (rows_q, d_sb), 0) // t_new
    lane_head = lax.broadcasted_iota(jnp.int32, (rows_q, d_sb), 1) // HEAD_DIM
    q_rows = jnp.concatenate([q_ref[0]] * n_heads, axis=0)
    qbd = jnp.where(row_head == lane_head, q_rows, 0.0).astype(_BF16)

    def visit(k_refs, v_refs, mask, carry):
        z = [_dot(qbd, r[0].reshape(d_sb, page).astype(_BF16)) + bias for r in k_refs]
        sp = [_softplus(x) for x in z]
        log_keep = [-x if mask is None else jnp.where(mask, -x, 0.0) for x in sp]
        totals = [jnp.sum(x, -1, keepdims=True) for x in log_keep]
        acc = None
        for r in reversed(range(len(z))):
            w = jnp.exp(z[r] - sp[r] + (_suffix_sums(log_keep[r], tri) + carry))
            if mask is not None:
                w = jnp.where(mask, w, 0.0)
            carry = carry + totals[r]
            o = _dot_nt(w.astype(_BF16), v_refs[r][0].reshape(d_sb, page).astype(_BF16))
            acc = o if acc is None else acc + o
        return carry, acc

    @pl.when(g == 0)
    def _():
        t_row = lax.broadcasted_iota(jnp.int32, (rows_q, page), 0) % t_new
        key = lax.broadcasted_iota(jnp.int32, (rows_q, page), 1)
        carry, acc = visit([kn_ref], [vn_ref], key < t_row, jnp.zeros((rows_q, 1), _F32))
        carry_ref[...] = carry
        acc_ref[...] = acc

    carry, acc = visit(page_refs[:pages], page_refs[pages:], None, carry_ref[...])
    carry_ref[...] = carry
    acc_ref[...] += acc

    @pl.when(g == pl.num_programs(1) - 1)
    def _():
        acc = acc_ref[...]
        lane_h = lax.broadcasted_iota(jnp.int32, (t_new, d_sb), 1) // HEAD_DIM
        out = jnp.zeros((t_new, d_sb), _F32)
        for hh in range(n_heads):
            out = jnp.where(lane_h == hh, acc[hh * t_new:(hh + 1) * t_new, :], out)
        o_ref[0] = out


def _sb_attend_sample(q, k_new, v_new, cache_k, cache_v, page_table, bias_rows, tri, batch, t_new):
    n, d_sb = q.shape
    n_heads = d_sb // HEAD_DIM
    n_pages = page_table.shape[1]
    page = cache_k.shape[1]
    pages = min(PAGES_PER_STEP, n_pages)
    assert page == LANES and n_pages % pages == 0 and t_new == SUBLANES
    groups = n_pages // pages
    ck = cache_k.transpose(0, 2, 3, 1)
    cv = cache_v.transpose(0, 2, 3, 1)
    as_page = lambda t: jnp.pad(t.reshape(batch, t_new, n_heads, HEAD_DIM).transpose(0, 2, 3, 1),
                                ((0, 0), (0, 0), (0, 0), (0, page - t_new)))
    q3, k3, v3 = q.reshape(batch, t_new, d_sb), as_page(k_new), as_page(v_new)
    tok = pl.BlockSpec((1, t_new, d_sb), lambda b, g, pt: (b, 0, 0))
    new_page = pl.BlockSpec((1, n_heads, HEAD_DIM, page), lambda b, g, pt: (b, 0, 0, 0))

    def page_spec(r):
        return pl.BlockSpec((1, n_heads, HEAD_DIM, page),
                            lambda b, g, pt: (pt[b, (groups - 1 - g) * pages + r], 0, 0, 0))

    out = pl.pallas_call(
        functools.partial(_sb_sample_kernel, t_new=t_new, n_heads=n_heads, pages=pages),
        out_shape=jax.ShapeDtypeStruct((batch, t_new, d_sb), _F32),
        grid_spec=pltpu.PrefetchScalarGridSpec(
            num_scalar_prefetch=1,
            grid=(batch, groups),
            in_specs=[tok, new_page, new_page,
                      pl.BlockSpec(bias_rows.shape, lambda b, g, pt: (0, 0)),
                      pl.BlockSpec(tri.shape, lambda b, g, pt: (0, 0))]
                     + [page_spec(r) for r in range(pages)] + [page_spec(r) for r in range(pages)],
            out_specs=tok,
            scratch_shapes=[pltpu.VMEM((n_heads * t_new, d_sb), _F32), pltpu.VMEM((n_heads * t_new, 1), _F32)],
        ),
        compiler_params=_params(("parallel", "arbitrary")),
        name="sb_sample",
    )(page_table, q3, k3, v3, bias_rows, tri, *([ck] * pages), *([cv] * pages))
    return out.reshape(n, d_sb)


def _seq_layout(tm, t_seq):
    return (tm // t_seq, t_seq) if tm >= t_seq else (1, tm)


def _boundary_rows(init_ref, carry_ref, tm, t_seq, depth):
    nseq, rows = _seq_layout(tm, t_seq)
    if nseq == 1:
        tiles_per_seq = t_seq // tm

        @pl.when(pl.program_id(0) % tiles_per_seq == 0)
        def _():
            carry_ref[...] = init_ref[0]

        c = carry_ref[...]
        return [jnp.broadcast_to(c[r:r + 1, :], (tm, c.shape[1])) for r in range(depth)]
    c = init_ref[...]
    d = c.shape[2]
    return [jnp.broadcast_to(c[:, r:r + 1, :], (nseq, rows, d)).reshape(tm, d) for r in range(depth)]


def _shifted(x, boundary, tm, t_seq, shift, depth):
    _, rows = _seq_layout(tm, t_seq)
    pos = lax.broadcasted_iota(jnp.int32, x.shape, 0) % rows
    out = pltpu.roll(x, shift, 0)
    for s in range(shift):
        out = jnp.where(pos == s, boundary[depth - (shift - s)], out)
    return out


def _init_spec(tm, t_seq, depth, d):
    nseq, _ = _seq_layout(tm, t_seq)
    if nseq == 1:
        tiles_per_seq = t_seq // tm
        return pl.BlockSpec((1, depth, d), lambda i: (i // tiles_per_seq, 0, 0))
    return pl.BlockSpec((nseq, depth, d), lambda i: (i, 0, 0))


def _rwkv_prep_kernel(p_ref, init_ref, mu_ref, w0_ref, a0_ref, w2_ref, a2_ref, g2_ref,
                      r_ref, k_ref, v_ref, w_ref, a_ref, g_ref, carry_ref, *, tm, t_seq, d_rw):
    p = p_ref[...]
    boundary = _boundary_rows(init_ref, carry_ref, tm, t_seq, 1)
    prev = _shifted(p, boundary, tm, t_seq, 1, 1)
    if _seq_layout(tm, t_seq)[0] == 1:
        carry_ref[...] = p[tm - 1:tm, :]
    xm = p + (prev - p) * mu_ref[...]
    r_ref[...] = xm[:, 0:d_rw]
    k_ref[...] = xm[:, d_rw:2 * d_rw]
    v_ref[...] = xm[:, 2 * d_rw:3 * d_rw]
    lora_wa = xm[:, 3 * d_rw:3 * d_rw + LORA_W + LORA_A]
    pg = xm[:, 3 * d_rw + LORA_W + LORA_A:]
    lw = _dot(jnp.tanh(lora_wa).astype(_BF16), w2_ref[...])
    la = _dot(lora_wa.astype(_BF16), a2_ref[...])
    log_w = -_softplus(-(w0_ref[...] + lw)) - 0.5
    w_ref[...] = jnp.exp(-jnp.exp(log_w))
    a_ref[...] = _sigmoid(a0_ref[...] + la)
    g_ref[...] = _dot(_sigmoid(pg).astype(_BF16), g2_ref[...])


def _rwkv_prep(prw, shift_init, mu, w0, a0, w2p, a2p, g2, t_seq):
    n, p_rw = prw.shape
    d_rw = w0.shape[1]
    tm = _token_tile(n, t_seq)
    row = lambda width: pl.BlockSpec((tm, width), lambda i: (i, 0))
    outs = tuple(jax.ShapeDtypeStruct((n, d_rw), _F32) for _ in range(6))
    return pl.pallas_call(
        functools.partial(_rwkv_prep_kernel, tm=tm, t_seq=t_seq, d_rw=d_rw),
        out_shape=outs,
        grid=(n // tm,),
        in_specs=[row(p_rw), _init_spec(tm, t_seq, 1, p_rw)] + [_resident()] * 6,
        out_specs=tuple(row(d_rw) for _ in range(6)),
        scratch_shapes=[pltpu.VMEM((1, p_rw), _F32)],
        compiler_params=_params(("arbitrary",)),
        name="rwkv_prep",
    )(prw, shift_init, mu, w0, a0, w2p, a2p, g2)


def _rwkv_scan_kernel(r_ref, k_ref, v_ref, w_ref, a_ref, kk_tab, ka_tab, rk_tab, lnw_tab, lnb_tab, s0_ref,
                      y_ref, s_out_ref, s_ref, kk_ref, b_ref, k2_ref, *, steps):
    c = pl.program_id(1)
    n = HEAD_DIM

    @pl.when(c == 0)
    def _():
        s_ref[...] = s0_ref[...]

    def col_sum(x):
        return jnp.sum(x, axis=0, keepdims=True)

    def step(t, _):
        r_t, k_t, v_t, a_t = r_ref[t], k_ref[t], v_ref[t], a_ref[t]
        kk = k_t * kk_tab[...]
        kk = kk * lax.rsqrt(jnp.maximum(col_sum(kk * kk), KK_EPS))
        k2 = k_t * (1.0 + (a_t - 1.0) * ka_tab[...])
        kk_ref[...] = kk
        b_ref[...] = kk * a_t
        k2_ref[...] = k2
        bonus = col_sum(r_t * k2 * rk_tab[...]) * v_t

        def dot_kk(j, acc):
            return acc + s_ref[j] * kk_ref[pl.ds(j, 1), :]

        s_kk = lax.fori_loop(0, n, dot_kk, jnp.zeros((n, LANES), _F32), unroll=8)

        def update(j, acc):
            row = pl.ds(j, 1)
            s_j = s_ref[j] * w_ref[t, row, :] - s_kk * b_ref[row, :] + v_t * k2_ref[row, :]
            s_ref[j] = s_j
            return acc + s_j * r_ref[t, row, :]

        y = lax.fori_loop(0, n, update, jnp.zeros((n, LANES), _F32), unroll=8)
        mean = col_sum(y) * (1.0 / n)
        yc = y - mean
        var = col_sum(yc * yc) * (1.0 / n)
        y_ref[t] = yc * lax.rsqrt(var + GN_EPS) * lnw_tab[...] + lnb_tab[...] + bonus
        return 0

    lax.fori_loop(0, steps, step, 0)

    @pl.when(c == pl.num_programs(1) - 1)
    def _():
        s_out_ref[...] = s_ref[...]


def _rwkv_scan(r, k, v, w, a, tabs, s0):
    t_len, n, lanes = r.shape
    steps = min(RWKV_TIME_CHUNK, t_len)
    assert t_len % steps == 0 and lanes % LANES == 0 and n == HEAD_DIM
    seq = pl.BlockSpec((steps, n, LANES), lambda g, c: (c, 0, g))
    tab = pl.BlockSpec((n, LANES), lambda g, c: (0, g))
    st = pl.BlockSpec((n, n, LANES), lambda g, c: (0, 0, g))
    return pl.pallas_call(
        functools.partial(_rwkv_scan_kernel, steps=steps),
        out_shape=(jax.ShapeDtypeStruct((t_len, n, lanes), _F32), jax.ShapeDtypeStruct((n, n, lanes), _F32)),
        grid=(lanes // LANES, t_len // steps),
        in_specs=[seq] * 5 + [tab] * 5 + [st],
        out_specs=(seq, st),
        scratch_shapes=[pltpu.VMEM((n, n, LANES), _F32)] + [pltpu.VMEM((n, LANES), _F32)] * 3,
        compiler_params=_params(("parallel", "arbitrary")),
        name="rwkv_scan",
    )(r, k, v, w, a, *tabs, s0)


def _to_lanes(x, batch, t_len, lanes):
    h = x.shape[1] // HEAD_DIM
    y = x.reshape(batch, t_len, h, HEAD_DIM).transpose(1, 3, 0, 2).reshape(t_len, HEAD_DIM, batch * h)
    return jnp.pad(y, ((0, 0), (0, 0), (0, lanes - batch * h)))


def _from_lanes(y, batch, heads):
    t_len = y.shape[0]
    y = y[:, :, :batch * heads].reshape(t_len, HEAD_DIM, batch, heads)
    return y.transpose(2, 0, 3, 1).reshape(batch * t_len, heads * HEAD_DIM)


def _table(x, batch, lanes):
    h = x.size // HEAD_DIM
    t = jnp.tile(x.reshape(h, HEAD_DIM).T, (1, batch))
    return jnp.pad(t, ((0, 0), (0, lanes - batch * h)))


def _out_ffn_kernel(*refs, gated, d_half):
    if gated:
        x_ref, ma_ref, y_ref, g_ref, wo_ref, gain_ref, wg_ref, wu_ref, wd_ref, o_ref = refs
        mb = (y_ref[...] * g_ref[...]).astype(_BF16)
    else:
        x_ref, ma_ref, mb_ref, wo_ref, gain_ref, wg_ref, wu_ref, wd_ref, o_ref = refs
        mb = mb_ref[...].astype(_BF16)
    ma = ma_ref[...].astype(_BF16)
    x1 = x_ref[...] + _dot(ma, wo_ref[0:d_half, :]) + _dot(mb, wo_ref[d_half:, :])
    h = _rmsnorm_rows(x1, gain_ref[...]).astype(_BF16)
    gate = _dot(h, wg_ref[...])
    act = (gate * _sigmoid(gate) * _dot(h, wu_ref[...])).astype(_BF16)
    o_ref[...] = x1 + _dot(act, wd_ref[...])


def _out_ffn(x, mix, wo_bf, gain, wg_bf, wu_bf, wd_bf, t_seq):
    n, d = x.shape
    d_half = wo_bf.shape[0] // 2
    tm = _token_tile(n, t_seq)
    gated = len(mix) == 3
    if gated or mix[0] is not mix[1]:
        mix_specs = [pl.BlockSpec((tm, d_half), lambda i: (i, 0))] * len(mix)
    else:
        mix_specs = [pl.BlockSpec((tm, d_half), lambda i: (i, 0)), pl.BlockSpec((tm, d_half), lambda i: (i, 1))]
    return pl.pallas_call(
        functools.partial(_out_ffn_kernel, gated=gated, d_half=d_half),
        out_shape=jax.ShapeDtypeStruct((n, d), _F32),
        grid=(n // tm,),
        in_specs=[pl.BlockSpec((tm, d), lambda i: (i, 0))] + mix_specs + [_resident()] * 5,
        out_specs=pl.BlockSpec((tm, d), lambda i: (i, 0)),
        compiler_params=_params(("parallel",)),
        name="out_ffn",
    )(x, *mix, wo_bf, gain, wg_bf, wu_bf, wd_bf)


def _conv_mix_kernel(x_ref, gain_ref, w_ref, cw_ref, init_ref, m_ref, state_ref, carry_ref, *, tm, t_seq, d_conv):
    h = _rmsnorm_rows(x_ref[...], gain_ref[...]).astype(_BF16)
    u = _dot(h, w_ref[:, d_conv:2 * d_conv]) * _dot(h, w_ref[:, 2 * d_conv:])
    depth = cw_ref.shape[0] - 1
    boundary = _boundary_rows(init_ref, carry_ref, tm, t_seq, depth)
    y = u * cw_ref[depth:depth + 1, :]
    for s in range(1, depth + 1):
        y = y + _shifted(u, boundary, tm, t_seq, s, depth) * cw_ref[depth - s:depth - s + 1, :]
    m_ref[...] = (_dot(h, w_ref[:, 0:d_conv]) * y).astype(m_ref.dtype)
    nseq, rows = _seq_layout(tm, t_seq)
    if nseq == 1:
        carry_ref[...] = u[tm - depth:, :]
        state_ref[0] = u[tm - depth:, :]
    else:
        state_ref[...] = u.reshape(nseq, rows, d_conv)[:, rows - depth:, :]


def _conv_mix(x, gain, w_bf, conv_w, init, t_seq):
    n, d = x.shape
    d_conv = conv_w.shape[1]
    depth = conv_w.shape[0] - 1
    tm = _token_tile(n, t_seq)
    return pl.pallas_call(
        functools.partial(_conv_mix_kernel, tm=tm, t_seq=t_seq, d_conv=d_conv),
        out_shape=(jax.ShapeDtypeStruct((n, d_conv), _BF16), jax.ShapeDtypeStruct(init.shape, _F32)),
        grid=(n // tm,),
        in_specs=[pl.BlockSpec((tm, d), lambda i: (i, 0)), _resident(), _resident(), _resident(),
                  _init_spec(tm, t_seq, depth, d_conv)],
        out_specs=(pl.BlockSpec((tm, d_conv), lambda i: (i, 0)), _init_spec(tm, t_seq, depth, d_conv)),
        scratch_shapes=[pltpu.VMEM((depth, d_conv), _F32)],
        compiler_params=_params(("arbitrary",)),
        name="conv_mix",
    )(x, gain, w_bf, conv_w, init)


def _strict_upper_sum_matrix(n):
    j = lax.broadcasted_iota(jnp.int32, (n, n), 0)
    s = lax.broadcasted_iota(jnp.int32, (n, n), 1)
    return (j > s).astype(_BF16)


def kernel(x_prompt, x_sample, cache_k, cache_v, state_wkv, state_shift, state_conv, page_table, norm_mix, norm_ffn, w_in_ab, q_norm, k_norm, sb_bias, mu_rw, w0, w2, a0, a2, g2, k_k, k_a, r_k, lnx_w, lnx_b, w_out_ab, w_in_c, conv_w, w_out_c, w_gate, w_up, w_down):
    bp, sp, d = x_prompt.shape
    bs, ts, _ = x_sample.shape
    depth = norm_mix.shape[0]
    d_rw = w0.shape[1]
    d_sb = w_out_ab.shape[1] - d_rw
    sb_heads, rw_heads = d_sb // HEAD_DIM, d_rw // HEAD_DIM
    p_rw = mu_rw.shape[1]
    groups = ((x_prompt.reshape(bp * sp, d), bp, sp), (x_sample.reshape(bs * ts, d), bs, ts))
    xs = [g[0] for g in groups]
    bf = lambda t: t.astype(_BF16)
    row = lambda t: t.reshape(1, -1)

    head_id = jnp.arange(d_sb) // HEAD_DIM
    seg = jnp.where(head_id[:, None] == head_id[None, :], 1.0 / HEAD_DIM, 0.0).astype(_BF16)
    tri_prompt = _strict_upper_sum_matrix(min(ATTN_TILE, sp))
    tri_page = _strict_upper_sum_matrix(cache_k.shape[2])

    outs = {name: ([], []) for name in ("k", "v", "wkv", "shift", "conv")}
    for layer in range(depth):
        j = layer // 2
        if layer % 2 == 0:
            w_in = bf(w_in_ab[j])
            qg, kg = row(jnp.tile(q_norm[j], sb_heads)), row(jnp.tile(k_norm[j], sb_heads))
            w2p = bf(jnp.concatenate([w2[j], jnp.zeros((LORA_A, d_rw), _F32)], 0))
            a2p = bf(jnp.concatenate([jnp.zeros((LORA_W, d_rw), _F32), a2[j]], 0))
            bias_rows = jnp.broadcast_to(jnp.repeat(sb_bias[j], ts)[:, None], (sb_heads * ts, LANES))
            for gi, (_, batch, t_len) in enumerate(groups):
                x = xs[gi]
                q, k, v, prw = _in_proj(x, row(norm_mix[layer]), w_in, qg, kg, seg, t_len)
                if gi == 0:
                    o_sb = _sb_attend_prompt(q, k, v, sb_bias[j], tri_prompt, batch, t_len)
                    shift_init = jnp.zeros((batch, 1, p_rw), _F32)
                    wkv0 = jnp.zeros((batch, rw_heads, HEAD_DIM, HEAD_DIM), _F32)
                else:
                    o_sb = _sb_attend_sample(q, k, v, cache_k[j], cache_v[j], page_table, bias_rows, tri_page,
                                             batch, t_len)
                    shift_init = state_shift[j][:, None, :]
                    wkv0 = state_wkv[j]
                r_, k_, v_, w_, a_, g_ = _rwkv_prep(prw, shift_init, row(mu_rw[j]), row(w0[j]), row(a0[j]),
                                                    w2p, a2p, bf(g2[j]), t_len)
                lanes = -(-batch * rw_heads // LANES) * LANES
                seqs = [_to_lanes(t, batch, t_len, lanes) for t in (r_, k_, v_, w_, a_)]
                tabs = [_table(t, batch, lanes) for t in (k_k[j], k_a[j], r_k[j], lnx_w[j], lnx_b[j])]
                s0 = jnp.pad(wkv0.transpose(3, 2, 0, 1).reshape(HEAD_DIM, HEAD_DIM, batch * rw_heads),
                             ((0, 0), (0, 0), (0, lanes - batch * rw_heads)))
                y_l, s_l = _rwkv_scan(*seqs, tabs, s0)
                y_rw = _from_lanes(y_l, batch, rw_heads)
                wkv = s_l[:, :, :batch * rw_heads].reshape(HEAD_DIM, HEAD_DIM, batch, rw_heads).transpose(2, 3, 1, 0)
                xs[gi] = _out_ffn(x, (o_sb, y_rw, g_), bf(w_out_ab[j]), row(norm_ffn[layer]),
                                  bf(w_gate[layer]), bf(w_up[layer]), bf(w_down[layer]), t_len)
                outs["k"][gi].append(k.reshape(batch, t_len, sb_heads, HEAD_DIM))
                outs["v"][gi].append(v.reshape(batch, t_len, sb_heads, HEAD_DIM))
                outs["wkv"][gi].append(wkv)
                outs["shift"][gi].append(prw.reshape(batch, t_len, p_rw)[:, -1])
        else:
            for gi, (_, batch, t_len) in enumerate(groups):
                x = xs[gi]
                init = jnp.zeros((batch, conv_w.shape[1] - 1, conv_w.shape[2]), _F32) if gi == 0 else state_conv[j]
                m, buf = _conv_mix(x, row(norm_mix[layer]), bf(w_in_c[j]), conv_w[j], init, t_len)
                xs[gi] = _out_ffn(x, (m, m), bf(w_out_c[j]), row(norm_ffn[layer]),
                                  bf(w_gate[layer]), bf(w_up[layer]), bf(w_down[layer]), t_len)
                outs["conv"][gi].append(buf)
    stack = lambda name, gi: jnp.stack(outs[name][gi])
    return (xs[0].reshape(bp, sp, d), xs[1].reshape(bs, ts, d),
            stack("k", 0), stack("v", 0), stack("k", 1), stack("v", 1),
            stack("wkv", 0), stack("wkv", 1), stack("shift", 0), stack("shift", 1),
            stack("conv", 0), stack("conv", 1))
```

```python
import functools

import jax
import jax.numpy as jnp
from jax import lax
from jax.experimental import pallas as pl
from jax.experimental.pallas import tpu as pltpu

HEAD_DIM = 64
LORA_W = 64
LORA_A = 64
LORA_G = 128
RMS_EPS = 1e-6
GN_EPS = 64e-5
KK_EPS = 1e-24

LANES = 128
SUBLANES = 8
TOKEN_TILE = 512
ATTN_TILE = 512
SUFFIX_CHUNK = 256
ATTN_HEADS = 4
RWKV_TIME_CHUNK = 32
PAGES_PER_STEP = 16
VMEM_LIMIT = 56 * 1024 * 1024

_F32 = jnp.float32
_BF16 = jnp.bfloat16


def _params(semantics):
    return pltpu.CompilerParams(dimension_semantics=semantics, vmem_limit_bytes=VMEM_LIMIT)


def _resident():
    return pl.BlockSpec(memory_space=pltpu.VMEM)


def _dot(a, b):
    return jnp.dot(a, b, preferred_element_type=_F32)


def _dot_nt(a, b):
    return lax.dot_general(a, b, (((1,), (1,)), ((), ())), preferred_element_type=_F32)


def _softplus(x):
    return jnp.maximum(x, 0.0) + jnp.log(1.0 + jnp.exp(-jnp.abs(x)))


def _sigmoid(x):
    return 1.0 / (1.0 + jnp.exp(-x))


def _rmsnorm_rows(x, gain):
    return x * lax.rsqrt(jnp.mean(x * x, -1, keepdims=True) + RMS_EPS) * gain


def _split_hi_lo(x):
    hi = x.astype(_BF16)
    lo = (x - hi.astype(_F32)).astype(_BF16)
    return hi, lo


def _token_tile(n, t_seq):
    tm = min(TOKEN_TILE, n)
    assert n % tm == 0 and (t_seq % tm == 0 or tm % t_seq == 0), (n, tm, t_seq)
    return tm


def _in_proj_kernel(x_ref, gain_ref, w_ref, qg_ref, kg_ref, seg_ref, q_ref, k_ref, v_ref, p_ref, *, d_sb):
    h = _rmsnorm_rows(x_ref[...], gain_ref[...]).astype(_BF16)
    seg = seg_ref[...]

    def head_norm(p, g):
        return p * lax.rsqrt(_dot((p * p).astype(_BF16), seg) + RMS_EPS) * g

    q = head_norm(_dot(h, w_ref[:, 0:d_sb]), qg_ref[...])
    q_ref[...] = q * (HEAD_DIM ** -0.5)
    k_ref[...] = head_norm(_dot(h, w_ref[:, d_sb:2 * d_sb]), kg_ref[...])
    v_ref[...] = _dot(h, w_ref[:, 2 * d_sb:3 * d_sb])
    p_ref[...] = _dot(h, w_ref[:, 3 * d_sb:])


def _in_proj(x, gain, w_bf, qg, kg, seg, t_seq):
    n, d = x.shape
    p_ab = w_bf.shape[1]
    d_sb = qg.shape[1]
    p_rw = p_ab - 3 * d_sb
    tm = _token_tile(n, t_seq)
    row = lambda width: pl.BlockSpec((tm, width), lambda i: (i, 0))
    return pl.pallas_call(
        functools.partial(_in_proj_kernel, d_sb=d_sb),
        out_shape=(jax.ShapeDtypeStruct((n, d_sb), _F32), jax.ShapeDtypeStruct((n, d_sb), _F32),
                   jax.ShapeDtypeStruct((n, d_sb), _F32), jax.ShapeDtypeStruct((n, p_rw), _F32)),
        grid=(n // tm,),
        in_specs=[row(d), _resident(), _resident(), _resident(), _resident(), _resident()],
        out_specs=(row(d_sb), row(d_sb), row(d_sb), row(p_rw)),
        compiler_params=_params(("parallel",)),
        name="in_proj",
    )(x, gain, w_bf, qg, kg, seg)


def _suffix_sums(log_keep, tri):
    m = log_keep.shape[0]
    r = _dot(jnp.concatenate(_split_hi_lo(log_keep), 0), tri)
    return r[:m] + r[m:]


def _sb_block(z, mask, carry, tri):
    sp = _softplus(z)
    log_keep = -sp
    log_sig = z - sp
    if mask is not None:
        log_keep = jnp.where(mask, log_keep, 0.0)
    chunk = tri.shape[0]
    after = []
    for c0 in reversed(range(0, z.shape[1], chunk)):
        part = log_keep[:, c0:c0 + chunk]
        after.append(_suffix_sums(part, tri) + carry)
        carry = carry + jnp.sum(part, -1, keepdims=True)
    w = jnp.exp(log_sig + jnp.concatenate(after[::-1], 1))
    if mask is not None:
        w = jnp.where(mask, w, 0.0)
    return w, carry


def _sb_prompt_kernel(bias_ref, q_ref, k_ref, v_ref, tri_ref, o_ref, *, tile, heads):
    group = pl.program_id(1)
    i = pl.program_id(2)
    tri = tri_ref[...]
    lane_head = lax.broadcasted_iota(jnp.int32, (tile, LANES), 1) // HEAD_DIM
    lanes_of = lambda hh: slice((hh // 2) * LANES, (hh // 2 + 1) * LANES)
    qh = []
    for hh in range(heads):
        q = q_ref[0, :, lanes_of(hh)].astype(_BF16)
        qh.append(jnp.where(lane_head == hh % 2, q, jnp.zeros_like(q)))
    bias = [bias_ref[group * heads + hh] for hh in range(heads)]

    def visit(j, state, mask):
        start = pl.multiple_of(j * tile, tile)
        out = []
        for hh in range(heads):
            kb = k_ref[0, pl.ds(start, tile), lanes_of(hh)].astype(_BF16)
            vb = v_ref[0, pl.ds(start, tile), lanes_of(hh)].astype(_BF16)
            carry, acc = state[hh]
            w, carry = _sb_block(_dot_nt(qh[hh], kb) + bias[hh], mask, carry, tri)
            out.append((carry, acc + _dot(w.astype(_BF16), vb)))
        return tuple(out)

    rows = lax.broadcasted_iota(jnp.int32, (tile, tile), 0)
    cols = lax.broadcasted_iota(jnp.int32, (tile, tile), 1)
    zero = (jnp.zeros((tile, 1), _F32), jnp.zeros((tile, LANES), _F32))
    state = visit(i, (zero,) * heads, cols < rows)
    state = lax.fori_loop(0, i, lambda n, s: visit(i - 1 - n, s, None), state)
    for p in range(heads // 2):
        o_ref[0, :, p * LANES:(p + 1) * LANES] = jnp.where(
            lane_head == 0, state[2 * p][1], state[2 * p + 1][1]).astype(o_ref.dtype)


def _sb_attend_prompt(q, k, v, bias, tri, batch, seq):
    n, d_sb = q.shape
    tile = min(ATTN_TILE, seq)
    width = ATTN_HEADS * HEAD_DIM
    assert seq % tile == 0 and d_sb % width == 0 and width % LANES == 0 and tile % tri.shape[0] == 0
    q3, k3, v3 = (t.reshape(batch, seq, d_sb) for t in (q, k, v))
    full = pl.BlockSpec((1, seq, width), lambda b, p, i: (b, 0, p))
    blk = pl.BlockSpec((1, tile, width), lambda b, p, i: (b, i, p))
    out = pl.pallas_call(
        functools.partial(_sb_prompt_kernel, tile=tile, heads=ATTN_HEADS),
        out_shape=jax.ShapeDtypeStruct((batch, seq, d_sb), _BF16),
        grid=(batch, d_sb // width, seq // tile),
        in_specs=[pl.BlockSpec(memory_space=pltpu.SMEM), blk, full, full, _resident()],
        out_specs=blk,
        compiler_params=_params(("parallel", "parallel", "arbitrary")),
        name="sb_prompt",
    )(bias, q3, k3, v3, tri)
    return out.reshape(n, d_sb)


def _sb_sample_kernel(pt_ref, q_ref, kn_ref, vn_ref, bias_ref, tri_ref, *rest, t_new, n_heads, pages):
    del pt_ref
    page_refs, (o_ref, acc_ref, carry_ref) = rest[:2 * pages], rest[2 * pages:]
    g = pl.program_id(1)
    rows_q = n_heads * t_new
    d_sb = n_heads * HEAD_DIM
    page = tri_ref.shape[0]
    tri = tri_ref[...]
    bias = bias_ref[...][:, 0:1]
    row_head = lax.broadcasted_iota(jnp.int32, (rows_q, d_sb), 0) // t_new
    lane_head = lax.broadcasted_iota(jnp.int32, (rows_q, d_sb), 1) // HEAD_DIM
    q_rows = jnp.concatenate([q_ref[0]] * n_heads, axis=0)
    qbd = jnp.where(row_head == lane_head, q_rows, 0.0).astype(_BF16)

    def visit(k_refs, v_refs, mask, carry):
        z = [_dot(qbd, r[0].reshape(d_sb, page).astype(_BF16)) + bias for r in k_refs]
        sp = [_softplus(x) for x in z]
        log_keep = [-x if mask is None else jnp.where(mask, -x, 0.0) for x in sp]
        totals = [jnp.sum(x, -1, keepdims=True) for x in log_keep]
        acc = None
        for r in reversed(range(len(z))):
            w = jnp.exp(z[r] - sp[r] + (_suffix_sums(log_keep[r], tri) + carry))
            if mask is not None:
                w = jnp.where(mask, w, 0.0)
            carry = carry + totals[r]
            o = _dot_nt(w.astype(_BF16), v_refs[r][0].reshape(d_sb, page).astype(_BF16))
            acc = o if acc is None else acc + o
        return carry, acc

    @pl.when(g == 0)
    def _():
        t_row = lax.broadcasted_iota(jnp.int32, (rows_q, page), 0) % t_new
        key = lax.broadcasted_iota(jnp.int32, (rows_q, page), 1)
        carry, acc = visit([kn_ref], [vn_ref], key < t_row, jnp.zeros((rows_q, 1), _F32))
        carry_ref[...] = carry
        acc_ref[...] = acc

    carry, acc = visit(page_refs[:pages], page_refs[pages:], None, carry_ref[...])
    carry_ref[...] = carry
    acc_ref[...] += acc

    @pl.when(g == pl.num_programs(1) - 1)
    def _():
        acc = acc_ref[...]
        lane_h = lax.broadcasted_iota(jnp.int32, (t_new, d_sb), 1) // HEAD_DIM
        out = jnp.zeros((t_new, d_sb), _F32)
        for hh in range(n_heads):
            out = jnp.where(lane_h == hh, acc[hh * t_new:(hh + 1) * t_new, :], out)
        o_ref[0] = out


def _sb_attend_sample(q, k_new, v_new, cache_k, cache_v, page_table, bias_rows, tri, batch, t_new):
    n, d_sb = q.shape
    n_heads = d_sb // HEAD_DIM
    n_pages = page_table.shape[1]
    page = cache_k.shape[1]
    pages = min(PAGES_PER_STEP, n_pages)
    assert page == LANES and n_pages % pages == 0 and t_new == SUBLANES
    groups = n_pages // pages
    ck = cache_k.transpose(0, 2, 3, 1)
    cv = cache_v.transpose(0, 2, 3, 1)
    as_page = lambda t: jnp.pad(t.reshape(batch, t_new, n_heads, HEAD_DIM).transpose(0, 2, 3, 1),
                                ((0, 0), (0, 0), (0, 0), (0, page - t_new)))
    q3, k3, v3 = q.reshape(batch, t_new, d_sb), as_page(k_new), as_page(v_new)
    tok = pl.BlockSpec((1, t_new, d_sb), lambda b, g, pt: (b, 0, 0))
    new_page = pl.BlockSpec((1, n_heads, HEAD_DIM, page), lambda b, g, pt: (b, 0, 0, 0))

    def page_spec(r):
        return pl.BlockSpec((1, n_heads, HEAD_DIM, page),
                            lambda b, g, pt: (pt[b, (groups - 1 - g) * pages + r], 0, 0, 0))

    out = pl.pallas_call(
        functools.partial(_sb_sample_kernel, t_new=t_new, n_heads=n_heads, pages=pages),
        out_shape=jax.ShapeDtypeStruct((batch, t_new, d_sb), _F32),
        grid_spec=pltpu.PrefetchScalarGridSpec(
            num_scalar_prefetch=1,
            grid=(batch, groups),
            in_specs=[tok, new_page, new_page,
                      pl.BlockSpec(bias_rows.shape, lambda b, g, pt: (0, 0)),
                      pl.BlockSpec(tri.shape, lambda b, g, pt: (0, 0))]
                     + [page_spec(r) for r in range(pages)] + [page_spec(r) for r in range(pages)],
            out_specs=tok,
            scratch_shapes=[pltpu.VMEM((n_heads * t_new, d_sb), _F32), pltpu.VMEM((n_heads * t_new, 1), _F32)],
        ),
        compiler_params=_params(("parallel", "arbitrary")),
        name="sb_sample",
    )(page_table, q3, k3, v3, bias_rows, tri, *([ck] * pages), *([cv] * pages))
    return out.reshape(n, d_sb)


def _seq_layout(tm, t_seq):
    return (tm // t_seq, t_seq) if tm >= t_seq else (1, tm)


def _boundary_rows(init_ref, carry_ref, tm, t_seq, depth):
    nseq, rows = _seq_layout(tm, t_seq)
    if nseq == 1:
        tiles_per_seq = t_seq // tm

        @pl.when(pl.program_id(0) % tiles_per_seq == 0)
        def _():
            carry_ref[...] = init_ref[0]

        c = carry_ref[...]
        return [jnp.broadcast_to(c[r:r + 1, :], (tm, c.shape[1])) for r in range(depth)]
    c = init_ref[...]
    d = c.shape[2]
    return [jnp.broadcast_to(c[:, r:r + 1, :], (nseq, rows, d)).reshape(tm, d) for r in range(depth)]


def _shifted(x, boundary, tm, t_seq, shift, depth):
    _, rows = _seq_layout(tm, t_seq)
    pos = lax.broadcasted_iota(jnp.int32, x.shape, 0) % rows
    out = pltpu.roll(x, shift, 0)
    for s in range(shift):
        out = jnp.where(pos == s, boundary[depth - (shift - s)], out)
    return out


def _init_spec(tm, t_seq, depth, d):
    nseq, _ = _seq_layout(tm, t_seq)
    if nseq == 1:
        tiles_per_seq = t_seq // tm
        return pl.BlockSpec((1, depth, d), lambda i: (i // tiles_per_seq, 0, 0))
    return pl.BlockSpec((nseq, depth, d), lambda i: (i, 0, 0))


def _rwkv_prep_kernel(p_ref, init_ref, mu_ref, w0_ref, a0_ref, w2_ref, a2_ref, g2_ref,
                      r_ref, k_ref, v_ref, w_ref, a_ref, g_ref, carry_ref, *, tm, t_seq, d_rw):
    p = p_ref[...]
    boundary = _boundary_rows(init_ref, carry_ref, tm, t_seq, 1)
    prev = _shifted(p, boundary, tm, t_seq, 1, 1)
    if _seq_layout(tm, t_seq)[0] == 1:
        carry_ref[...] = p[tm - 1:tm, :]
    xm = p + (prev - p) * mu_ref[...]
    r_ref[...] = xm[:, 0:d_rw]
    k_ref[...] = xm[:, d_rw:2 * d_rw]
    v_ref[...] = xm[:, 2 * d_rw:3 * d_rw]
    lora_wa = xm[:, 3 * d_rw:3 * d_rw + LORA_W + LORA_A]
    pg = xm[:, 3 * d_rw + LORA_W + LORA_A:]
    lw = _dot(jnp.tanh(lora_wa).astype(_BF16), w2_ref[...])
    la = _dot(lora_wa.astype(_BF16), a2_ref[...])
    log_w = -_softplus(-(w0_ref[...] + lw)) - 0.5
    w_ref[...] = jnp.exp(-jnp.exp(log_w))
    a_ref[...] = _sigmoid(a0_ref[...] + la)
    g_ref[...] = _dot(_sigmoid(pg).astype(_BF16), g2_ref[...])


def _time_major(tm, t_seq):
    return tm <= t_seq


def _time_major_spec(tm, t_seq, width):
    tiles_per_seq = t_seq // tm
    return pl.BlockSpec((tm, width), lambda i: (i % tiles_per_seq, i // tiles_per_seq))


def _rwkv_prep(prw, shift_init, mu, w0, a0, w2p, a2p, g2, t_seq):
    n, p_rw = prw.shape
    d_rw = w0.shape[1]
    tm = _token_tile(n, t_seq)
    row = lambda width: pl.BlockSpec((tm, width), lambda i: (i, 0))
    if _time_major(tm, t_seq):
        seq_shape, seq_spec = (t_seq, n // t_seq * d_rw), _time_major_spec(tm, t_seq, d_rw)
    else:
        seq_shape, seq_spec = (n, d_rw), row(d_rw)
    outs = tuple(jax.ShapeDtypeStruct(seq_shape, _F32) for _ in range(5)) + (jax.ShapeDtypeStruct((n, d_rw), _F32),)
    return pl.pallas_call(
        functools.partial(_rwkv_prep_kernel, tm=tm, t_seq=t_seq, d_rw=d_rw),
        out_shape=outs,
        grid=(n // tm,),
        in_specs=[row(p_rw), _init_spec(tm, t_seq, 1, p_rw)] + [_resident()] * 6,
        out_specs=(seq_spec,) * 5 + (row(d_rw),),
        scratch_shapes=[pltpu.VMEM((1, p_rw), _F32)],
        compiler_params=_params(("arbitrary",)),
        name="rwkv_prep",
    )(prw, shift_init, mu, w0, a0, w2p, a2p, g2)


def _rwkv_scan_kernel(r_ref, k_ref, v_ref, w_ref, a_ref, kk_tab, ka_tab, rk_tab, lnw_tab, lnb_tab, s0_ref,
                      y_ref, s_out_ref, s_ref, kk_ref, b_ref, k2_ref, *, steps):
    c = pl.program_id(1)
    n = HEAD_DIM

    @pl.when(c == 0)
    def _():
        s_ref[...] = s0_ref[...]

    def col_sum(x):
        return jnp.sum(x, axis=0, keepdims=True)

    def step(t, _):
        r_t, k_t, v_t, a_t = r_ref[t], k_ref[t], v_ref[t], a_ref[t]
        kk = k_t * kk_tab[...]
        kk = kk * lax.rsqrt(jnp.maximum(col_sum(kk * kk), KK_EPS))
        k2 = k_t * (1.0 + (a_t - 1.0) * ka_tab[...])
        kk_ref[...] = kk
        b_ref[...] = kk * a_t
        k2_ref[...] = k2
        bonus = col_sum(r_t * k2 * rk_tab[...]) * v_t

        def dot_kk(j, acc):
            return acc + s_ref[j] * kk_ref[pl.ds(j, 1), :]

        s_kk = lax.fori_loop(0, n, dot_kk, jnp.zeros((n, LANES), _F32), unroll=8)

        def update(j, acc):
            row = pl.ds(j, 1)
            s_j = s_ref[j] * w_ref[t, row, :] - s_kk * b_ref[row, :] + v_t * k2_ref[row, :]
            s_ref[j] = s_j
            return acc + s_j * r_ref[t, row, :]

        y = lax.fori_loop(0, n, update, jnp.zeros((n, LANES), _F32), unroll=8)
        mean = col_sum(y) * (1.0 / n)
        yc = y - mean
        var = col_sum(yc * yc) * (1.0 / n)
        y_ref[t] = yc * lax.rsqrt(var + GN_EPS) * lnw_tab[...] + lnb_tab[...] + bonus
        return 0

    lax.fori_loop(0, steps, step, 0)

    @pl.when(c == pl.num_programs(1) - 1)
    def _():
        s_out_ref[...] = s_ref[...]


def _rwkv_scan(r, k, v, w, a, tabs, s0):
    t_len, n, lanes = r.shape
    steps = min(RWKV_TIME_CHUNK, t_len)
    assert t_len % steps == 0 and lanes % LANES == 0 and n == HEAD_DIM
    seq = pl.BlockSpec((steps, n, LANES), lambda g, c: (c, 0, g))
    tab = pl.BlockSpec((n, LANES), lambda g, c: (0, g))
    st = pl.BlockSpec((n, n, LANES), lambda g, c: (0, 0, g))
    return pl.pallas_call(
        functools.partial(_rwkv_scan_kernel, steps=steps),
        out_shape=(jax.ShapeDtypeStruct((t_len, n, lanes), _F32), jax.ShapeDtypeStruct((n, n, lanes), _F32)),
        grid=(lanes // LANES, t_len // steps),
        in_specs=[seq] * 5 + [tab] * 5 + [st],
        out_specs=(seq, st),
        scratch_shapes=[pltpu.VMEM((n, n, LANES), _F32)] + [pltpu.VMEM((n, LANES), _F32)] * 3,
        compiler_params=_params(("parallel", "arbitrary")),
        name="rwkv_scan",
    )(r, k, v, w, a, *tabs, s0)


def _to_lanes(x, batch, t_len, lanes):
    if x.shape[0] == t_len and batch > 1:
        y = jnp.swapaxes(x.reshape(t_len, -1, HEAD_DIM), 1, 2)
    else:
        h = x.shape[1] // HEAD_DIM
        y = x.reshape(batch, t_len, h, HEAD_DIM).transpose(1, 3, 0, 2).reshape(t_len, HEAD_DIM, batch * h)
    return jnp.pad(y, ((0, 0), (0, 0), (0, lanes - y.shape[2])))


def _from_lanes(y, batch, heads, time_major):
    t_len = y.shape[0]
    y = y[:, :, :batch * heads]
    if time_major:
        return jnp.swapaxes(y, 1, 2).reshape(t_len, batch * heads * HEAD_DIM)
    return y.reshape(t_len, HEAD_DIM, batch, heads).transpose(2, 0, 3, 1).reshape(batch * t_len, heads * HEAD_DIM)


def _table(x, batch, lanes):
    h = x.size // HEAD_DIM
    t = jnp.tile(x.reshape(h, HEAD_DIM).T, (1, batch))
    return jnp.pad(t, ((0, 0), (0, lanes - batch * h)))


def _out_ffn_kernel(*refs, gated, d_half):
    if gated:
        x_ref, ma_ref, y_ref, g_ref, wo_ref, gain_ref, wg_ref, wu_ref, wd_ref, o_ref = refs
        mb = (y_ref[...] * g_ref[...]).astype(_BF16)
    else:
        x_ref, ma_ref, mb_ref, wo_ref, gain_ref, wg_ref, wu_ref, wd_ref, o_ref = refs
        mb = mb_ref[...].astype(_BF16)
    ma = ma_ref[...].astype(_BF16)
    x1 = x_ref[...] + _dot(ma, wo_ref[0:d_half, :]) + _dot(mb, wo_ref[d_half:, :])
    h = _rmsnorm_rows(x1, gain_ref[...]).astype(_BF16)
    gate = _dot(h, wg_ref[...])
    act = (gate * _sigmoid(gate) * _dot(h, wu_ref[...])).astype(_BF16)
    o_ref[...] = x1 + _dot(act, wd_ref[...])


def _out_ffn(x, mix, wo_bf, gain, wg_bf, wu_bf, wd_bf, t_seq):
    n, d = x.shape
    d_half = wo_bf.shape[0] // 2
    tm = _token_tile(n, t_seq)
    gated = len(mix) == 3
    if gated:
        row = pl.BlockSpec((tm, d_half), lambda i: (i, 0))
        y_spec = row if mix[1].shape == (n, d_half) else _time_major_spec(tm, t_seq, d_half)
        mix_specs = [row, y_spec, row]
    elif mix[0] is not mix[1]:
        mix_specs = [pl.BlockSpec((tm, d_half), lambda i: (i, 0))] * len(mix)
    else:
        mix_specs = [pl.BlockSpec((tm, d_half), lambda i: (i, 0)), pl.BlockSpec((tm, d_half), lambda i: (i, 1))]
    return pl.pallas_call(
        functools.partial(_out_ffn_kernel, gated=gated, d_half=d_half),
        out_shape=jax.ShapeDtypeStruct((n, d), _F32),
        grid=(n // tm,),
        in_specs=[pl.BlockSpec((tm, d), lambda i: (i, 0))] + mix_specs + [_resident()] * 5,
        out_specs=pl.BlockSpec((tm, d), lambda i: (i, 0)),
        compiler_params=_params(("parallel",)),
        name="out_ffn",
    )(x, *mix, wo_bf, gain, wg_bf, wu_bf, wd_bf)


def _conv_mix_kernel(x_ref, gain_ref, w_ref, cw_ref, init_ref, m_ref, state_ref, carry_ref, *, tm, t_seq, d_conv):
    h = _rmsnorm_rows(x_ref[...], gain_ref[...]).astype(_BF16)
    u = _dot(h, w_ref[:, d_conv:2 * d_conv]) * _dot(h, w_ref[:, 2 * d_conv:])
    depth = cw_ref.shape[0] - 1
    boundary = _boundary_rows(init_ref, carry_ref, tm, t_seq, depth)
    y = u * cw_ref[depth:depth + 1, :]
    for s in range(1, depth + 1):
        y = y + _shifted(u, boundary, tm, t_seq, s, depth) * cw_ref[depth - s:depth - s + 1, :]
    m_ref[...] = (_dot(h, w_ref[:, 0:d_conv]) * y).astype(m_ref.dtype)
    nseq, rows = _seq_layout(tm, t_seq)
    if nseq == 1:
        carry_ref[...] = u[tm - depth:, :]
        state_ref[0] = u[tm - depth:, :]
    else:
        state_ref[...] = u.reshape(nseq, rows, d_conv)[:, rows - depth:, :]


def _conv_mix(x, gain, w_bf, conv_w, init, t_seq):
    n, d = x.shape
    d_conv = conv_w.shape[1]
    depth = conv_w.shape[0] - 1
    tm = _token_tile(n, t_seq)
    return pl.pallas_call(
        functools.partial(_conv_mix_kernel, tm=tm, t_seq=t_seq, d_conv=d_conv),
        out_shape=(jax.ShapeDtypeStruct((n, d_conv), _BF16), jax.ShapeDtypeStruct(init.shape, _F32)),
        grid=(n // tm,),
        in_specs=[pl.BlockSpec((tm, d), lambda i: (i, 0)), _resident(), _resident(), _resident(),
                  _init_spec(tm, t_seq, depth, d_conv)],
        out_specs=(pl.BlockSpec((tm, d_conv), lambda i: (i, 0)), _init_spec(tm, t_seq, depth, d_conv)),
        scratch_shapes=[pltpu.VMEM((depth, d_conv), _F32)],
        compiler_params=_params(("arbitrary",)),
        name="conv_mix",
    )(x, gain, w_bf, conv_w, init)


def _strict_upper_sum_matrix(n):
    j = lax.broadcasted_iota(jnp.int32, (n, n), 0)
    s = lax.broadcasted_iota(jnp.int32, (n, n), 1)
    return (j > s).astype(_BF16)


def kernel(x_prompt, x_sample, cache_k, cache_v, state_wkv, state_shift, state_conv, page_table, norm_mix, norm_ffn, w_in_ab, q_norm, k_norm, sb_bias, mu_rw, w0, w2, a0, a2, g2, k_k, k_a, r_k, lnx_w, lnx_b, w_out_ab, w_in_c, conv_w, w_out_c, w_gate, w_up, w_down):
    bp, sp, d = x_prompt.shape
    bs, ts, _ = x_sample.shape
    depth = norm_mix.shape[0]
    d_rw = w0.shape[1]
    d_sb = w_out_ab.shape[1] - d_rw
    sb_heads, rw_heads = d_sb // HEAD_DIM, d_rw // HEAD_DIM
    p_rw = mu_rw.shape[1]
    groups = ((x_prompt.reshape(bp * sp, d), bp, sp), (x_sample.reshape(bs * ts, d), bs, ts))
    xs = [g[0] for g in groups]
    bf = lambda t: t.astype(_BF16)
    row = lambda t: t.reshape(1, -1)

    head_id = jnp.arange(d_sb) // HEAD_DIM
    seg = jnp.where(head_id[:, None] == head_id[None, :], 1.0 / HEAD_DIM, 0.0).astype(_BF16)
    tri_prompt = _strict_upper_sum_matrix(min(SUFFIX_CHUNK, ATTN_TILE, sp))
    tri_page = _strict_upper_sum_matrix(cache_k.shape[2])

    outs = {name: ([], []) for name in ("k", "v", "wkv", "shift", "conv")}
    for layer in range(depth):
        j = layer // 2
        if layer % 2 == 0:
            w_in = bf(w_in_ab[j])
            qg, kg = row(jnp.tile(q_norm[j], sb_heads)), row(jnp.tile(k_norm[j], sb_heads))
            w2p = bf(jnp.concatenate([w2[j], jnp.zeros((LORA_A, d_rw), _F32)], 0))
            a2p = bf(jnp.concatenate([jnp.zeros((LORA_W, d_rw), _F32), a2[j]], 0))
            bias_rows = jnp.broadcast_to(jnp.repeat(sb_bias[j], ts)[:, None], (sb_heads * ts, LANES))
            for gi, (_, batch, t_len) in enumerate(groups):
                x = xs[gi]
                q, k, v, prw = _in_proj(x, row(norm_mix[layer]), w_in, qg, kg, seg, t_len)
                if gi == 0:
                    o_sb = _sb_attend_prompt(q, k, v, sb_bias[j], tri_prompt, batch, t_len)
                    shift_init = jnp.zeros((batch, 1, p_rw), _F32)
                    wkv0 = jnp.zeros((batch, rw_heads, HEAD_DIM, HEAD_DIM), _F32)
                else:
                    o_sb = _sb_attend_sample(q, k, v, cache_k[j], cache_v[j], page_table, bias_rows, tri_page,
                                             batch, t_len)
                    shift_init = state_shift[j][:, None, :]
                    wkv0 = state_wkv[j]
                r_, k_, v_, w_, a_, g_ = _rwkv_prep(prw, shift_init, row(mu_rw[j]), row(w0[j]), row(a0[j]),
                                                    w2p, a2p, bf(g2[j]), t_len)
                lanes = -(-batch * rw_heads // LANES) * LANES
                seqs = [_to_lanes(t, batch, t_len, lanes) for t in (r_, k_, v_, w_, a_)]
                tabs = [_table(t, batch, lanes) for t in (k_k[j], k_a[j], r_k[j], lnx_w[j], lnx_b[j])]
                s0 = jnp.pad(wkv0.transpose(3, 2, 0, 1).reshape(HEAD_DIM, HEAD_DIM, batch * rw_heads),
                             ((0, 0), (0, 0), (0, lanes - batch * rw_heads)))
                y_l, s_l = _rwkv_scan(*seqs, tabs, s0)
                y_rw = _from_lanes(y_l, batch, rw_heads, time_major=r_.shape[0] != x.shape[0])
                wkv = s_l[:, :, :batch * rw_heads].reshape(HEAD_DIM, HEAD_DIM, batch, rw_heads).transpose(2, 3, 1, 0)
                xs[gi] = _out_ffn(x, (o_sb, y_rw, g_), bf(w_out_ab[j]), row(norm_ffn[layer]),
                                  bf(w_gate[layer]), bf(w_up[layer]), bf(w_down[layer]), t_len)
                outs["k"][gi].append(k.reshape(batch, t_len, sb_heads, HEAD_DIM))
                outs["v"][gi].append(v.reshape(batch, t_len, sb_heads, HEAD_DIM))
                outs["wkv"][gi].append(wkv)
                outs["shift"][gi].append(prw.reshape(batch, t_len, p_rw)[:, -1])
        else:
            for gi, (_, batch, t_len) in enumerate(groups):
                x = xs[gi]
                init = jnp.zeros((batch, conv_w.shape[1] - 1, conv_w.shape[2]), _F32) if gi == 0 else state_conv[j]
                m, buf = _conv_mix(x, row(norm_mix[layer]), bf(w_in_c[j]), conv_w[j], init, t_len)
                xs[gi] = _out_ffn(x, (m, m), bf(w_out_c[j]), row(norm_ffn[layer]),
                                  bf(w_gate[layer]), bf(w_up[layer]), bf(w_down[layer]), t_len)
                outs["conv"][gi].append(buf)
    stack = lambda name, gi: jnp.stack(outs[name][gi])
    return (xs[0].reshape(bp, sp, d), xs[1].reshape(bs, ts, d),
            stack("k", 0), stack("v", 0), stack("k", 1), stack("v", 1),
            stack("wkv", 0), stack("wkv", 1), stack("shift", 0), stack("shift", 1),
            stack("conv", 0), stack("conv", 1))
```

```python
import functools

import jax
import jax.numpy as jnp
from jax import lax
from jax.experimental import pallas as pl
from jax.experimental.pallas import tpu as pltpu

HEAD_DIM = 64
LORA_W = 64
LORA_A = 64
LORA_G = 128
RMS_EPS = 1e-6
GN_EPS = 64e-5
KK_EPS = 1e-24

LANES = 128
SUBLANES = 8
TOKEN_TILE = 512
ATTN_TILE = 512
SUFFIX_CHUNK = 256
ATTN_HEADS = 8
RWKV_TIME_CHUNK = 32
SCAN_UNROLL = 32
PAGES_PER_STEP = 32
PAGES_PER_DOT = 16
VMEM_LIMIT =56 * 1024 * 1024

_F32 = jnp.float32
_BF16 = jnp.bfloat16


def _params(semantics):
    return pltpu.CompilerParams(dimension_semantics=semantics, vmem_limit_bytes=VMEM_LIMIT)


def _resident():
    return pl.BlockSpec(memory_space=pltpu.VMEM)


def _dot(a, b):
    return jnp.dot(a, b, preferred_element_type=_F32)


def _dot_nt(a, b):
    return lax.dot_general(a, b, (((1,), (1,)), ((), ())), preferred_element_type=_F32)


def _softplus(x):
    return jnp.maximum(x, 0.0) + jnp.log(1.0 + jnp.exp(-jnp.abs(x)))


def _sigmoid(x):
    return 1.0 / (1.0 + jnp.exp(-x))


def _rmsnorm_rows(x, gain):
    return x * lax.rsqrt(jnp.mean(x * x, -1, keepdims=True) + RMS_EPS) * gain


def _split_hi_lo(x):
    hi = x.astype(_BF16)
    lo = (x - hi.astype(_F32)).astype(_BF16)
    return hi, lo


def _token_tile(n, t_seq):
    tm = min(TOKEN_TILE, n)
    assert n % tm == 0 and (t_seq % tm == 0 or tm % t_seq == 0), (n, tm, t_seq)
    return tm


def _in_proj_kernel(x_ref, gain_ref, w_ref, qg_ref, kg_ref, seg_ref, q_ref, k_ref, v_ref, p_ref, *, d_sb):
    h = _rmsnorm_rows(x_ref[...], gain_ref[...]).astype(_BF16)
    seg = seg_ref[...]

    def head_norm(p, g):
        return p * lax.rsqrt(_dot((p * p).astype(_BF16), seg) + RMS_EPS) * g

    q = head_norm(_dot(h, w_ref[:, 0:d_sb]), qg_ref[...])
    q_ref[...] = q * (HEAD_DIM ** -0.5)
    k_ref[...] = head_norm(_dot(h, w_ref[:, d_sb:2 * d_sb]), kg_ref[...])
    v_ref[...] = _dot(h, w_ref[:, 2 * d_sb:3 * d_sb])
    p_ref[...] = _dot(h, w_ref[:, 3 * d_sb:])


def _in_proj(x, gain, w_bf, qg, kg, seg, t_seq):
    n, d = x.shape
    p_ab = w_bf.shape[1]
    d_sb = qg.shape[1]
    p_rw = p_ab - 3 * d_sb
    tm = _token_tile(n, t_seq)
    row = lambda width: pl.BlockSpec((tm, width), lambda i: (i, 0))
    return pl.pallas_call(
        functools.partial(_in_proj_kernel, d_sb=d_sb),
        out_shape=(jax.ShapeDtypeStruct((n, d_sb), _F32), jax.ShapeDtypeStruct((n, d_sb), _F32),
                   jax.ShapeDtypeStruct((n, d_sb), _F32), jax.ShapeDtypeStruct((n, p_rw), _F32)),
        grid=(n // tm,),
        in_specs=[row(d), _resident(), _resident(), _resident(), _resident(), _resident()],
        out_specs=(row(d_sb), row(d_sb), row(d_sb), row(p_rw)),
        compiler_params=_params(("parallel",)),
        name="in_proj",
    )(x, gain, w_bf, qg, kg, seg)


def _suffix_sums(log_keep, tri):
    m = log_keep.shape[0]
    r = _dot(jnp.concatenate(_split_hi_lo(log_keep), 0), tri)
    return r[:m] + r[m:]


def _sb_block(z, mask, carry, tri):
    sp = _softplus(z)
    log_keep = -sp
    log_sig = z - sp
    if mask is not None:
        log_keep = jnp.where(mask, log_keep, 0.0)
    chunk = tri.shape[0]
    after = []
    for c0 in reversed(range(0, z.shape[1], chunk)):
        part = log_keep[:, c0:c0 + chunk]
        after.append(_suffix_sums(part, tri) + carry)
        carry = carry + jnp.sum(part, -1, keepdims=True)
    w = jnp.exp(log_sig + jnp.concatenate(after[::-1], 1))
    if mask is not None:
        w = jnp.where(mask, w, 0.0)
    return w, carry


def _sb_prompt_kernel(bias_ref, q_ref, k_ref, v_ref, tri_ref, o_ref, *, tile, heads):
    group = pl.program_id(1)
    i = pl.program_id(2)
    tri = tri_ref[...]
    lane_head = lax.broadcasted_iota(jnp.int32, (tile, LANES), 1) // HEAD_DIM
    lanes_of = lambda hh: slice((hh // 2) * LANES, (hh // 2 + 1) * LANES)
    qh = []
    for hh in range(heads):
        q = q_ref[0, :, lanes_of(hh)].astype(_BF16)
        qh.append(jnp.where(lane_head == hh % 2, q, jnp.zeros_like(q)))
    bias = [bias_ref[group * heads + hh] for hh in range(heads)]

    def visit(j, state, mask):
        start = pl.multiple_of(j * tile, tile)
        out = []
        for hh in range(heads):
            kb = k_ref[0, pl.ds(start, tile), lanes_of(hh)].astype(_BF16)
            vb = v_ref[0, pl.ds(start, tile), lanes_of(hh)].astype(_BF16)
            carry, acc = state[hh]
            w, carry = _sb_block(_dot_nt(qh[hh], kb) + bias[hh], mask, carry, tri)
            out.append((carry, acc + _dot(w.astype(_BF16), vb)))
        return tuple(out)

    rows = lax.broadcasted_iota(jnp.int32, (tile, tile), 0)
    cols = lax.broadcasted_iota(jnp.int32, (tile, tile), 1)
    zero = (jnp.zeros((tile, 1), _F32), jnp.zeros((tile, LANES), _F32))
    state = visit(i, (zero,) * heads, cols < rows)
    state = lax.fori_loop(0, i, lambda n, s: visit(i - 1 - n, s, None), state)
    for p in range(heads // 2):
        o_ref[0, :, p * LANES:(p + 1) * LANES] = jnp.where(
            lane_head == 0, state[2 * p][1], state[2 * p + 1][1]).astype(o_ref.dtype)


def _sb_attend_prompt(q, k, v, bias, tri, batch, seq):
    n, d_sb = q.shape
    tile = min(ATTN_TILE, seq)
    width = ATTN_HEADS * HEAD_DIM
    assert seq % tile == 0 and d_sb % width == 0 and width % LANES == 0 and tile % tri.shape[0] == 0
    q3, k3, v3 = (t.reshape(batch, seq, d_sb) for t in (q, k, v))
    full = pl.BlockSpec((1, seq, width), lambda b, p, i: (b, 0, p))
    blk = pl.BlockSpec((1, tile, width), lambda b, p, i: (b, i, p))
    out = pl.pallas_call(
        functools.partial(_sb_prompt_kernel, tile=tile, heads=ATTN_HEADS),
        out_shape=jax.ShapeDtypeStruct((batch, seq, d_sb), _BF16),
        grid=(batch, d_sb // width, seq // tile),
        in_specs=[pl.BlockSpec(memory_space=pltpu.SMEM), blk, full, full, _resident()],
        out_specs=blk,
        compiler_params=_params(("parallel", "parallel", "arbitrary")),
        name="sb_prompt",
    )(bias, q3, k3, v3, tri)
    return out.reshape(n, d_sb)


def _sb_sample_kernel(pt_ref, q_ref, kn_ref, vn_ref, bias_ref, tri_ref, *rest, t_new, n_heads, pages):
    del pt_ref
    page_refs, (o_ref, acc_ref, carry_ref) = rest[:2 * pages], rest[2 * pages:]
    g = pl.program_id(1)
    rows_q = n_heads * t_new
    d_sb = n_heads * HEAD_DIM
    page = tri_ref.shape[0]
    tri = tri_ref[...]
    bias = bias_ref[...][:, 0:1]
    row_head = lax.broadcasted_iota(jnp.int32, (rows_q, d_sb), 0) // t_new
    lane_head = lax.broadcasted_iota(jnp.int32, (rows_q, d_sb), 1) // HEAD_DIM
    q_rows = jnp.concatenate([q_ref[0]] * n_heads, axis=0)
    qbd = jnp.where(row_head == lane_head, q_rows, 0.0).astype(_BF16)

    def visit(k_refs, v_refs, mask, carry):
        side_by_side = lambda refs: jnp.concatenate([r[0].reshape(d_sb, page).astype(_BF16) for r in refs], 1)
        acc = None
        for r0 in reversed(range(0, len(k_refs), PAGES_PER_DOT)):
            z = _dot(qbd, side_by_side(k_refs[r0:r0 + PAGES_PER_DOT])) + bias
            w, carry = _sb_block(z, mask, carry, tri)
            o = _dot_nt(w.astype(_BF16), side_by_side(v_refs[r0:r0 + PAGES_PER_DOT]))
            acc = o if acc is None else acc + o
        return carry, acc

    @pl.when(g == 0)
    def _():
        t_row = lax.broadcasted_iota(jnp.int32, (rows_q, page), 0) % t_new
        key = lax.broadcasted_iota(jnp.int32, (rows_q, page), 1)
        carry, acc = visit([kn_ref], [vn_ref], key < t_row, jnp.zeros((rows_q, 1), _F32))
        carry_ref[...] = carry
        acc_ref[...] = acc

    carry, acc = visit(page_refs[:pages], page_refs[pages:], None, carry_ref[...])
    carry_ref[...] = carry
    acc_ref[...] += acc

    @pl.when(g == pl.num_programs(1) - 1)
    def _():
        acc = acc_ref[...]
        lane_h = lax.broadcasted_iota(jnp.int32, (t_new, d_sb), 1) // HEAD_DIM
        out = jnp.zeros((t_new, d_sb), _F32)
        for hh in range(n_heads):
            out = jnp.where(lane_h == hh, acc[hh * t_new:(hh + 1) * t_new, :], out)
        o_ref[0] = out


def _sb_attend_sample(q, k_new, v_new, cache_k, cache_v, page_table, bias_rows, tri, batch, t_new):
    n, d_sb = q.shape
    n_heads = d_sb // HEAD_DIM
    n_pages = page_table.shape[1]
    page = cache_k.shape[1]
    pages = min(PAGES_PER_STEP, n_pages)
    assert page == LANES and n_pages % pages == 0 and t_new == SUBLANES
    groups = n_pages // pages
    ck = cache_k.transpose(0, 2, 3, 1)
    cv = cache_v.transpose(0, 2, 3, 1)
    as_page = lambda t: jnp.pad(t.reshape(batch, t_new, n_heads, HEAD_DIM).transpose(0, 2, 3, 1),
                                ((0, 0), (0, 0), (0, 0), (0, page - t_new)))
    q3, k3, v3 = q.reshape(batch, t_new, d_sb), as_page(k_new), as_page(v_new)
    tok = pl.BlockSpec((1, t_new, d_sb), lambda b, g, pt: (b, 0, 0))
    new_page = pl.BlockSpec((1, n_heads, HEAD_DIM, page), lambda b, g, pt: (b, 0, 0, 0))

    def page_spec(r):
        return pl.BlockSpec((1, n_heads, HEAD_DIM, page),
                            lambda b, g, pt: (pt[b, (groups - 1 - g) * pages + r], 0, 0, 0))

    out = pl.pallas_call(
        functools.partial(_sb_sample_kernel, t_new=t_new, n_heads=n_heads, pages=pages),
        out_shape=jax.ShapeDtypeStruct((batch, t_new, d_sb), _F32),
        grid_spec=pltpu.PrefetchScalarGridSpec(
            num_scalar_prefetch=1,
            grid=(batch, groups),
            in_specs=[tok, new_page, new_page,
                      pl.BlockSpec(bias_rows.shape, lambda b, g, pt: (0, 0)),
                      pl.BlockSpec(tri.shape, lambda b, g, pt: (0, 0))]
                     + [page_spec(r) for r in range(pages)] + [page_spec(r) for r in range(pages)],
            out_specs=tok,
            scratch_shapes=[pltpu.VMEM((n_heads * t_new, d_sb), _F32), pltpu.VMEM((n_heads * t_new, 1), _F32)],
        ),
        compiler_params=_params(("parallel", "arbitrary")),
        name="sb_sample",
    )(page_table, q3, k3, v3, bias_rows, tri, *([ck] * pages), *([cv] * pages))
    return out.reshape(n, d_sb)


def _seq_layout(tm, t_seq):
    return (tm // t_seq, t_seq) if tm >= t_seq else (1, tm)


def _boundary_rows(init_ref, carry_ref, tm, t_seq, depth):
    nseq, rows = _seq_layout(tm, t_seq)
    if nseq == 1:
        tiles_per_seq = t_seq // tm

        @pl.when(pl.program_id(0) % tiles_per_seq == 0)
        def _():
            carry_ref[...] = init_ref[0]

        c = carry_ref[...]
        return [jnp.broadcast_to(c[r:r + 1, :], (tm, c.shape[1])) for r in range(depth)]
    c = init_ref[...]
    d = c.shape[2]
    return [jnp.broadcast_to(c[:, r:r + 1, :], (nseq, rows, d)).reshape(tm, d) for r in range(depth)]


def _shifted(x, boundary, tm, t_seq, shift, depth):
    _, rows = _seq_layout(tm, t_seq)
    pos = lax.broadcasted_iota(jnp.int32, x.shape, 0) % rows
    out = pltpu.roll(x, shift, 0)
    for s in range(shift):
        out = jnp.where(pos == s, boundary[depth - (shift - s)], out)
    return out


def _init_spec(tm, t_seq, depth, d):
    nseq, _ = _seq_layout(tm, t_seq)
    if nseq == 1:
        tiles_per_seq = t_seq // tm
        return pl.BlockSpec((1, depth, d), lambda i: (i // tiles_per_seq, 0, 0))
    return pl.BlockSpec((nseq, depth, d), lambda i: (i, 0, 0))


def _rwkv_prep_kernel(p_ref, init_ref, mu_ref, w0_ref, a0_ref, w2_ref, a2_ref, g2_ref,
                      r_ref, k_ref, v_ref, w_ref, a_ref, g_ref, carry_ref, *, tm, t_seq, d_rw):
    p = p_ref[...]
    boundary = _boundary_rows(init_ref, carry_ref, tm, t_seq, 1)
    prev = _shifted(p, boundary, tm, t_seq, 1, 1)
    if _seq_layout(tm, t_seq)[0] == 1:
        carry_ref[...] = p[tm - 1:tm, :]
    xm = p + (prev - p) * mu_ref[...]
    r_ref[...] = xm[:, 0:d_rw]
    k_ref[...] = xm[:, d_rw:2 * d_rw]
    v_ref[...] = xm[:, 2 * d_rw:3 * d_rw]
    lora_wa = xm[:, 3 * d_rw:3 * d_rw + LORA_W + LORA_A]
    pg = xm[:, 3 * d_rw + LORA_W + LORA_A:]
    lw = _dot(jnp.tanh(lora_wa).astype(_BF16), w2_ref[...])
    la = _dot(lora_wa.astype(_BF16), a2_ref[...])
    log_w = -_softplus(-(w0_ref[...] + lw)) - 0.5
    w_ref[...] = jnp.exp(-jnp.exp(log_w))
    a_ref[...] = _sigmoid(a0_ref[...] + la)
    g_ref[...] = _dot(_sigmoid(pg).astype(_BF16), g2_ref[...])


def _time_major(tm, t_seq):
    return tm <= t_seq


def _time_major_spec(tm, t_seq, width):
    tiles_per_seq = t_seq // tm
    return pl.BlockSpec((tm, width), lambda i: (i % tiles_per_seq, i // tiles_per_seq))


def _rwkv_prep(prw, shift_init, mu, w0, a0, w2p, a2p, g2, t_seq):
    n, p_rw = prw.shape
    d_rw = w0.shape[1]
    tm = _token_tile(n, t_seq)
    row = lambda width: pl.BlockSpec((tm, width), lambda i: (i, 0))
    if _time_major(tm, t_seq):
        seq_shape, seq_spec = (t_seq, n // t_seq * d_rw), _time_major_spec(tm, t_seq, d_rw)
    else:
        seq_shape, seq_spec = (n, d_rw), row(d_rw)
    outs = tuple(jax.ShapeDtypeStruct(seq_shape, _F32) for _ in range(5)) + (jax.ShapeDtypeStruct((n, d_rw), _F32),)
    return pl.pallas_call(
        functools.partial(_rwkv_prep_kernel, tm=tm, t_seq=t_seq, d_rw=d_rw),
        out_shape=outs,
        grid=(n // tm,),
        in_specs=[row(p_rw), _init_spec(tm, t_seq, 1, p_rw)] + [_resident()] * 6,
        out_specs=(seq_spec,) * 5 + (row(d_rw),),
        scratch_shapes=[pltpu.VMEM((1, p_rw), _F32)],
        compiler_params=_params(("arbitrary",)),
        name="rwkv_prep",
    )(prw, shift_init, mu, w0, a0, w2p, a2p, g2)


def _rwkv_scan_kernel(r_ref, k_ref, v_ref, w_ref, a_ref, kk_tab, ka_tab, rk_tab, lnw_tab, lnb_tab, s0_ref,
                      y_ref, s_out_ref, s_ref, kk_ref, b_ref, k2_ref, *, steps):
    c = pl.program_id(1)
    n = HEAD_DIM

    @pl.when(c == 0)
    def _():
        s_ref[...] = s0_ref[...]

    def col_sum(x):
        return jnp.sum(x, axis=0, keepdims=True)

    def step(t, _):
        r_t, k_t, v_t, a_t = r_ref[t], k_ref[t], v_ref[t], a_ref[t]
        kk = k_t * kk_tab[...]
        kk = kk * lax.rsqrt(jnp.maximum(col_sum(kk * kk), KK_EPS))
        k2 = k_t * (1.0 + (a_t - 1.0) * ka_tab[...])
        kk_ref[...] = kk
        b_ref[...] = kk * a_t
        k2_ref[...] = k2
        bonus = col_sum(r_t * k2 * rk_tab[...]) * v_t

        def dot_kk(j, acc):
            return acc + s_ref[j] * kk_ref[pl.ds(j, 1), :]

        s_kk = lax.fori_loop(0, n, dot_kk, jnp.zeros((n, LANES), _F32), unroll=SCAN_UNROLL)

        def update(j, acc):
            row = pl.ds(j, 1)
            s_j = s_ref[j] * w_ref[t, row, :] - s_kk * b_ref[row, :] + v_t * k2_ref[row, :]
            s_ref[j] = s_j
            return acc + s_j * r_ref[t, row, :]

        y = lax.fori_loop(0, n, update, jnp.zeros((n, LANES), _F32), unroll=SCAN_UNROLL)
        mean = col_sum(y) * (1.0 / n)
        yc = y - mean
        var = col_sum(yc * yc) * (1.0 / n)
        y_ref[t] = yc * lax.rsqrt(var + GN_EPS) * lnw_tab[...] + lnb_tab[...] + bonus
        return 0

    lax.fori_loop(0, steps, step, 0)

    @pl.when(c == pl.num_programs(1) - 1)
    def _():
        s_out_ref[...] = s_ref[...]


def _rwkv_scan(r, k, v, w, a, tabs, s0):
    t_len, n, lanes = r.shape
    steps = min(RWKV_TIME_CHUNK, t_len)
    assert t_len % steps == 0 and lanes % LANES == 0 and n == HEAD_DIM
    seq = pl.BlockSpec((steps, n, LANES), lambda g, c: (c, 0, g))
    tab = pl.BlockSpec((n, LANES), lambda g, c: (0, g))
    st = pl.BlockSpec((n, n, LANES), lambda g, c: (0, 0, g))
    return pl.pallas_call(
        functools.partial(_rwkv_scan_kernel, steps=steps),
        out_shape=(jax.ShapeDtypeStruct((t_len, n, lanes), _F32), jax.ShapeDtypeStruct((n, n, lanes), _F32)),
        grid=(lanes // LANES, t_len // steps),
        in_specs=[seq] * 5 + [tab] * 5 + [st],
        out_specs=(seq, st),
        scratch_shapes=[pltpu.VMEM((n, n, LANES), _F32)] + [pltpu.VMEM((n, LANES), _F32)] * 3,
        compiler_params=_params(("parallel", "arbitrary")),
        name="rwkv_scan",
    )(r, k, v, w, a, *tabs, s0)


def _to_lanes(x, batch, t_len, lanes):
    if x.shape[0] == t_len and batch > 1:
        y = jnp.swapaxes(x.reshape(t_len, -1, HEAD_DIM), 1, 2)
    else:
        h = x.shape[1] // HEAD_DIM
        y = x.reshape(batch, t_len, h, HEAD_DIM).transpose(1, 3, 0, 2).reshape(t_len, HEAD_DIM, batch * h)
    return jnp.pad(y, ((0, 0), (0, 0), (0, lanes - y.shape[2])))


def _from_lanes(y, batch, heads, time_major):
    t_len = y.shape[0]
    y = y[:, :, :batch * heads]
    if time_major:
        return jnp.swapaxes(y, 1, 2).reshape(t_len, batch * heads * HEAD_DIM)
    return y.reshape(t_len, HEAD_DIM, batch, heads).transpose(2, 0, 3, 1).reshape(batch * t_len, heads * HEAD_DIM)


def _table(x, batch, lanes):
    h = x.size // HEAD_DIM
    t = jnp.tile(x.reshape(h, HEAD_DIM).T, (1, batch))
    return jnp.pad(t, ((0, 0), (0, lanes - batch * h)))


def _out_ffn_kernel(*refs, gated, d_half):
    if gated:
        x_ref, ma_ref, y_ref, g_ref, wo_ref, gain_ref, wg_ref, wu_ref, wd_ref, o_ref = refs
        mb = (y_ref[...] * g_ref[...]).astype(_BF16)
    else:
        x_ref, ma_ref, mb_ref, wo_ref, gain_ref, wg_ref, wu_ref, wd_ref, o_ref = refs
        mb = mb_ref[...].astype(_BF16)
    ma = ma_ref[...].astype(_BF16)
    x1 = x_ref[...] + _dot(ma, wo_ref[0:d_half, :]) + _dot(mb, wo_ref[d_half:, :])
    h = _rmsnorm_rows(x1, gain_ref[...]).astype(_BF16)
    gate = _dot(h, wg_ref[...])
    act = (gate * _sigmoid(gate) * _dot(h, wu_ref[...])).astype(_BF16)
    o_ref[...] = x1 + _dot(act, wd_ref[...])


def _out_ffn(x, mix, wo_bf, gain, wg_bf, wu_bf, wd_bf, t_seq):
    n, d = x.shape
    d_half = wo_bf.shape[0] // 2
    tm = _token_tile(n, t_seq)
    gated = len(mix) == 3
    if gated:
        row = pl.BlockSpec((tm, d_half), lambda i: (i, 0))
        y_spec = row if mix[1].shape == (n, d_half) else _time_major_spec(tm, t_seq, d_half)
        mix_specs = [row, y_spec, row]
    elif mix[0] is not mix[1]:
        mix_specs = [pl.BlockSpec((tm, d_half), lambda i: (i, 0))] * len(mix)
    else:
        mix_specs = [pl.BlockSpec((tm, d_half), lambda i: (i, 0)), pl.BlockSpec((tm, d_half), lambda i: (i, 1))]
    return pl.pallas_call(
        functools.partial(_out_ffn_kernel, gated=gated, d_half=d_half),
        out_shape=jax.ShapeDtypeStruct((n, d), _F32),
        grid=(n // tm,),
        in_specs=[pl.BlockSpec((tm, d), lambda i: (i, 0))] + mix_specs + [_resident()] * 5,
        out_specs=pl.BlockSpec((tm, d), lambda i: (i, 0)),
        compiler_params=_params(("parallel",)),
        name="out_ffn",
    )(x, *mix, wo_bf, gain, wg_bf, wu_bf, wd_bf)


def _conv_mix_kernel(x_ref, gain_ref, w_ref, cw_ref, init_ref, m_ref, state_ref, carry_ref, *, tm, t_seq, d_conv):
    h = _rmsnorm_rows(x_ref[...], gain_ref[...]).astype(_BF16)
    u = _dot(h, w_ref[:, d_conv:2 * d_conv]) * _dot(h, w_ref[:, 2 * d_conv:])
    depth = cw_ref.shape[0] - 1
    boundary = _boundary_rows(init_ref, carry_ref, tm, t_seq, depth)
    y = u * cw_ref[depth:depth + 1, :]
    for s in range(1, depth + 1):
        y = y + _shifted(u, boundary, tm, t_seq, s, depth) * cw_ref[depth - s:depth - s + 1, :]
    m_ref[...] = (_dot(h, w_ref[:, 0:d_conv]) * y).astype(m_ref.dtype)
    nseq, rows = _seq_layout(tm, t_seq)
    if nseq == 1:
        carry_ref[...] = u[tm - depth:, :]
        state_ref[0] = u[tm - depth:, :]
    else:
        state_ref[...] = u.reshape(nseq, rows, d_conv)[:, rows - depth:, :]


def _conv_mix(x, gain, w_bf, conv_w, init, t_seq):
    n, d = x.shape
    d_conv = conv_w.shape[1]
    depth = conv_w.shape[0] - 1
    tm = _token_tile(n, t_seq)
    return pl.pallas_call(
        functools.partial(_conv_mix_kernel, tm=tm, t_seq=t_seq, d_conv=d_conv),
        out_shape=(jax.ShapeDtypeStruct((n, d_conv), _BF16), jax.ShapeDtypeStruct(init.shape, _F32)),
        grid=(n // tm,),
        in_specs=[pl.BlockSpec((tm, d), lambda i: (i, 0)), _resident(), _resident(), _resident(),
                  _init_spec(tm, t_seq, depth, d_conv)],
        out_specs=(pl.BlockSpec((tm, d_conv), lambda i: (i, 0)), _init_spec(tm, t_seq, depth, d_conv)),
        scratch_shapes=[pltpu.VMEM((depth, d_conv), _F32)],
        compiler_params=_params(("arbitrary",)),
        name="conv_mix",
    )(x, gain, w_bf, conv_w, init)


def _strict_upper_sum_matrix(n):
    j = lax.broadcasted_iota(jnp.int32, (n, n), 0)
    s = lax.broadcasted_iota(jnp.int32, (n, n), 1)
    return (j > s).astype(_BF16)


def kernel(x_prompt, x_sample, cache_k, cache_v, state_wkv, state_shift, state_conv, page_table, norm_mix, norm_ffn, w_in_ab, q_norm, k_norm, sb_bias, mu_rw, w0, w2, a0, a2, g2, k_k, k_a, r_k, lnx_w, lnx_b, w_out_ab, w_in_c, conv_w, w_out_c, w_gate, w_up, w_down):
    bp, sp, d = x_prompt.shape
    bs, ts, _ = x_sample.shape
    depth = norm_mix.shape[0]
    d_rw = w0.shape[1]
    d_sb = w_out_ab.shape[1] - d_rw
    sb_heads, rw_heads = d_sb // HEAD_DIM, d_rw // HEAD_DIM
    p_rw = mu_rw.shape[1]
    groups = ((x_prompt.reshape(bp * sp, d), bp, sp), (x_sample.reshape(bs * ts, d), bs, ts))
    xs = [g[0] for g in groups]
    bf = lambda t: t.astype(_BF16)
    row = lambda t: t.reshape(1, -1)

    head_id = jnp.arange(d_sb) // HEAD_DIM
    seg = jnp.where(head_id[:, None] == head_id[None, :], 1.0 / HEAD_DIM, 0.0).astype(_BF16)
    tri_prompt = _strict_upper_sum_matrix(min(SUFFIX_CHUNK, ATTN_TILE, sp))
    tri_page = _strict_upper_sum_matrix(cache_k.shape[2])

    outs = {name: ([], []) for name in ("k", "v", "wkv", "shift", "conv")}
    for layer in range(depth):
        j = layer // 2
        if layer % 2 == 0:
            w_in = bf(w_in_ab[j])
            qg, kg = row(jnp.tile(q_norm[j], sb_heads)), row(jnp.tile(k_norm[j], sb_heads))
            w2p = bf(jnp.concatenate([w2[j], jnp.zeros((LORA_A, d_rw), _F32)], 0))
            a2p = bf(jnp.concatenate([jnp.zeros((LORA_W, d_rw), _F32), a2[j]], 0))
            bias_rows = jnp.broadcast_to(jnp.repeat(sb_bias[j], ts)[:, None], (sb_heads * ts, LANES))
            for gi, (_, batch, t_len) in enumerate(groups):
                x = xs[gi]
                q, k, v, prw = _in_proj(x, row(norm_mix[layer]), w_in, qg, kg, seg, t_len)
                if gi == 0:
                    o_sb = _sb_attend_prompt(q, k, v, sb_bias[j], tri_prompt, batch, t_len)
                    shift_init = jnp.zeros((batch, 1, p_rw), _F32)
                    wkv0 = jnp.zeros((batch, rw_heads, HEAD_DIM, HEAD_DIM), _F32)
                else:
                    o_sb = _sb_attend_sample(q, k, v, cache_k[j], cache_v[j], page_table, bias_rows, tri_page,
                                             batch, t_len)
                    shift_init = state_shift[j][:, None, :]
                    wkv0 = state_wkv[j]
                r_, k_, v_, w_, a_, g_ = _rwkv_prep(prw, shift_init, row(mu_rw[j]), row(w0[j]), row(a0[j]),
                                                    w2p, a2p, bf(g2[j]), t_len)
                lanes = -(-batch * rw_heads // LANES) * LANES
                seqs = [_to_lanes(t, batch, t_len, lanes) for t in (r_, k_, v_, w_, a_)]
                tabs = [_table(t, batch, lanes) for t in (k_k[j], k_a[j], r_k[j], lnx_w[j], lnx_b[j])]
                s0 = jnp.pad(wkv0.transpose(3, 2, 0, 1).reshape(HEAD_DIM, HEAD_DIM, batch * rw_heads),
                             ((0, 0), (0, 0), (0, lanes - batch * rw_heads)))
                y_l, s_l = _rwkv_scan(*seqs, tabs, s0)
                y_rw = _from_lanes(y_l, batch, rw_heads, time_major=r_.shape[0] != x.shape[0])
                wkv = s_l[:, :, :batch * rw_heads].reshape(HEAD_DIM, HEAD_DIM, batch, rw_heads).transpose(2, 3, 1, 0)
                xs[gi] = _out_ffn(x, (o_sb, y_rw, g_), bf(w_out_ab[j]), row(norm_ffn[layer]),
                                  bf(w_gate[layer]), bf(w_up[layer]), bf(w_down[layer]), t_len)
                outs["k"][gi].append(k.reshape(batch, t_len, sb_heads, HEAD_DIM))
                outs["v"][gi].append(v.reshape(batch, t_len, sb_heads, HEAD_DIM))
                outs["wkv"][gi].append(wkv)
                outs["shift"][gi].append(prw.reshape(batch, t_len, p_rw)[:, -1])
        else:
            for gi, (_, batch, t_len) in enumerate(groups):
                x = xs[gi]
                init = jnp.zeros((batch, conv_w.shape[1] - 1, conv_w.shape[2]), _F32) if gi == 0 else state_conv[j]
                m, buf = _conv_mix(x, row(norm_mix[layer]), bf(w_in_c[j]), conv_w[j], init, t_len)
                xs[gi] = _out_ffn(x, (m, m), bf(w_out_c[j]), row(norm_ffn[layer]),
                                  bf(w_gate[layer]), bf(w_up[layer]), bf(w_down[layer]), t_len)
                outs["conv"][gi].append(buf)
    stack = lambda name, gi: jnp.stack(outs[name][gi])
    return (xs[0].reshape(bp, sp, d), xs[1].reshape(bs, ts, d),
            stack("k", 0), stack("v", 0), stack("k", 1), stack("v", 1),
            stack("wkv", 0), stack("wkv", 1), stack("shift", 0), stack("shift", 1),
            stack("conv", 0), stack("conv", 1))
```

```python
import functools

import jax
import jax.numpy as jnp
from jax import lax
from jax.experimental import pallas as pl
from jax.experimental.pallas import tpu as pltpu

HEAD_DIM = 64
LORA_W = 64
LORA_A = 64
LORA_G = 128
RMS_EPS = 1e-6
GN_EPS = 64e-5
KK_EPS = 1e-24

LANES = 128
SUBLANES = 8
TOKEN_TILE = 512
ATTN_TILE = 512
SUFFIX_CHUNK = 256
ATTN_HEADS = 8
RWKV_TIME_CHUNK = 32
SCAN_UNROLL = 32
PAGES_PER_STEP = 32
PAGES_PER_DOT = 16
VMEM_LIMIT =56 * 1024 * 1024

_F32 = jnp.float32
_BF16 = jnp.bfloat16


def _params(semantics):
    return pltpu.CompilerParams(dimension_semantics=semantics, vmem_limit_bytes=VMEM_LIMIT)


def _resident():
    return pl.BlockSpec(memory_space=pltpu.VMEM)


def _dot(a, b):
    return jnp.dot(a, b, preferred_element_type=_F32)


def _dot_nt(a, b):
    return lax.dot_general(a, b, (((1,), (1,)), ((), ())), preferred_element_type=_F32)


def _softplus(x):
    return jnp.maximum(x, 0.0) + jnp.log(1.0 + jnp.exp(-jnp.abs(x)))


def _sigmoid(x):
    return 1.0 / (1.0 + jnp.exp(-x))


def _rmsnorm_rows(x, gain):
    return x * lax.rsqrt(jnp.mean(x * x, -1, keepdims=True) + RMS_EPS) * gain


def _split_hi_lo(x):
    hi = x.astype(_BF16)
    lo = (x - hi.astype(_F32)).astype(_BF16)
    return hi, lo


def _token_tile(n, t_seq):
    tm = min(TOKEN_TILE, n)
    assert n % tm == 0 and (t_seq % tm == 0 or tm % t_seq == 0), (n, tm, t_seq)
    return tm


def _in_proj_kernel(x_ref, gain_ref, w_ref, qg_ref, kg_ref, seg_ref, q_ref, k_ref, v_ref, p_ref, *, d_sb):
    h = _rmsnorm_rows(x_ref[...], gain_ref[...]).astype(_BF16)
    seg = seg_ref[...]

    def head_norm(p, g):
        return p * lax.rsqrt(_dot((p * p).astype(_BF16), seg) + RMS_EPS) * g

    q = head_norm(_dot(h, w_ref[:, 0:d_sb]), qg_ref[...])
    q_ref[...] = q * (HEAD_DIM ** -0.5)
    k_ref[...] = head_norm(_dot(h, w_ref[:, d_sb:2 * d_sb]), kg_ref[...])
    v_ref[...] = _dot(h, w_ref[:, 2 * d_sb:3 * d_sb])
    p_ref[...] = _dot(h, w_ref[:, 3 * d_sb:])


def _in_proj(x, gain, w_bf, qg, kg, seg, t_seq):
    n, d = x.shape
    p_ab = w_bf.shape[1]
    d_sb = qg.shape[1]
    p_rw = p_ab - 3 * d_sb
    tm = _token_tile(n, t_seq)
    row = lambda width: pl.BlockSpec((tm, width), lambda i: (i, 0))
    return pl.pallas_call(
        functools.partial(_in_proj_kernel, d_sb=d_sb),
        out_shape=(jax.ShapeDtypeStruct((n, d_sb), _F32), jax.ShapeDtypeStruct((n, d_sb), _F32),
                   jax.ShapeDtypeStruct((n, d_sb), _F32), jax.ShapeDtypeStruct((n, p_rw), _F32)),
        grid=(n // tm,),
        in_specs=[row(d), _resident(), _resident(), _resident(), _resident(), _resident()],
        out_specs=(row(d_sb), row(d_sb), row(d_sb), row(p_rw)),
        compiler_params=_params(("parallel",)),
        name="in_proj",
    )(x, gain, w_bf, qg, kg, seg)


def _suffix_sums(log_keep, tri):
    m = log_keep.shape[0]
    r = _dot(jnp.concatenate(_split_hi_lo(log_keep), 0), tri)
    return r[:m] + r[m:]


def _sb_block(neg_z, mask, carry, tri):
    log_keep = jnp.minimum(neg_z, 0.0) - jnp.log(1.0 + jnp.exp(-jnp.abs(neg_z)))
    log_sig = log_keep - neg_z
    if mask is not None:
        log_keep = jnp.where(mask, log_keep, 0.0)
    chunk = tri.shape[0]
    after = []
    for c0 in reversed(range(0, neg_z.shape[1], chunk)):
        part = log_keep[:, c0:c0 + chunk]
        after.append(_suffix_sums(part, tri) + carry)
        carry = carry + jnp.sum(part, -1, keepdims=True)
    w = jnp.exp(log_sig + jnp.concatenate(after[::-1], 1))
    if mask is not None:
        w = jnp.where(mask, w, 0.0)
    return w, carry


def _sb_prompt_kernel(bias_ref, q_ref, k_ref, v_ref, tri_ref, o_ref, *, tile, heads):
    group = pl.program_id(1)
    i = pl.program_id(2)
    tri = tri_ref[...]
    lane_head = lax.broadcasted_iota(jnp.int32, (tile, LANES), 1) // HEAD_DIM
    lanes_of = lambda hh: slice((hh // 2) * LANES, (hh // 2 + 1) * LANES)
    qh = []
    for hh in range(heads):
        q = (-q_ref[0, :, lanes_of(hh)]).astype(_BF16)
        qh.append(jnp.where(lane_head == hh % 2, q, jnp.zeros_like(q)))
    neg_bias = [-bias_ref[group * heads + hh] for hh in range(heads)]

    def visit(j, state, mask):
        start = pl.multiple_of(j * tile, tile)
        out = []
        for hh in range(heads):
            kb = k_ref[0, pl.ds(start, tile), lanes_of(hh)].astype(_BF16)
            vb = v_ref[0, pl.ds(start, tile), lanes_of(hh)].astype(_BF16)
            carry, acc = state[hh]
            w, carry = _sb_block(_dot_nt(qh[hh], kb) + neg_bias[hh], mask, carry, tri)
            out.append((carry, acc + _dot(w.astype(_BF16), vb)))
        return tuple(out)

    rows = lax.broadcasted_iota(jnp.int32, (tile, tile), 0)
    cols = lax.broadcasted_iota(jnp.int32, (tile, tile), 1)
    zero = (jnp.zeros((tile, 1), _F32), jnp.zeros((tile, LANES), _F32))
    state = visit(i, (zero,) * heads, cols < rows)
    state = lax.fori_loop(0, i, lambda n, s: visit(i - 1 - n, s, None), state)
    for p in range(heads // 2):
        o_ref[0, :, p * LANES:(p + 1) * LANES] = jnp.where(
            lane_head == 0, state[2 * p][1], state[2 * p + 1][1]).astype(o_ref.dtype)


def _sb_attend_prompt(q, k, v, bias, tri, batch, seq):
    n, d_sb = q.shape
    tile = min(ATTN_TILE, seq)
    width = ATTN_HEADS * HEAD_DIM
    assert seq % tile == 0 and d_sb % width == 0 and width % LANES == 0 and tile % tri.shape[0] == 0
    q3, k3, v3 = (t.reshape(batch, seq, d_sb) for t in (q, k, v))
    full = pl.BlockSpec((1, seq, width), lambda b, p, i: (b, 0, p))
    blk = pl.BlockSpec((1, tile, width), lambda b, p, i: (b, i, p))
    out = pl.pallas_call(
        functools.partial(_sb_prompt_kernel, tile=tile, heads=ATTN_HEADS),
        out_shape=jax.ShapeDtypeStruct((batch, seq, d_sb), _BF16),
        grid=(batch, d_sb // width, seq // tile),
        in_specs=[pl.BlockSpec(memory_space=pltpu.SMEM), blk, full, full, _resident()],
        out_specs=blk,
        compiler_params=_params(("parallel", "parallel", "arbitrary")),
        name="sb_prompt",
    )(bias, q3, k3, v3, tri)
    return out.reshape(n, d_sb)


def _sb_sample_kernel(pt_ref, q_ref, kn_ref, vn_ref, bias_ref, tri_ref, *rest, t_new, n_heads, pages):
    del pt_ref
    page_refs, (o_ref, acc_ref, carry_ref) = rest[:2 * pages], rest[2 * pages:]
    g = pl.program_id(1)
    rows_q = n_heads * t_new
    d_sb = n_heads * HEAD_DIM
    page = tri_ref.shape[0]
    tri = tri_ref[...]
    neg_bias = -bias_ref[...][:, 0:1]
    row_head = lax.broadcasted_iota(jnp.int32, (rows_q, d_sb), 0) // t_new
    lane_head = lax.broadcasted_iota(jnp.int32, (rows_q, d_sb), 1) // HEAD_DIM
    q_rows = jnp.concatenate([q_ref[0]] * n_heads, axis=0)
    qbd = jnp.where(row_head == lane_head, -q_rows, 0.0).astype(_BF16)

    def visit(k_refs, v_refs, mask, carry):
        side_by_side = lambda refs: jnp.concatenate([r[0].reshape(d_sb, page).astype(_BF16) for r in refs], 1)
        acc = None
        for r0 in reversed(range(0, len(k_refs), PAGES_PER_DOT)):
            neg_z = _dot(qbd, side_by_side(k_refs[r0:r0 + PAGES_PER_DOT])) + neg_bias
            w, carry = _sb_block(neg_z, mask, carry, tri)
            o = _dot_nt(w.astype(_BF16), side_by_side(v_refs[r0:r0 + PAGES_PER_DOT]))
            acc = o if acc is None else acc + o
        return carry, acc

    @pl.when(g == 0)
    def _():
        t_row = lax.broadcasted_iota(jnp.int32, (rows_q, page), 0) % t_new
        key = lax.broadcasted_iota(jnp.int32, (rows_q, page), 1)
        carry, acc = visit([kn_ref], [vn_ref], key < t_row, jnp.zeros((rows_q, 1), _F32))
        carry_ref[...] = carry
        acc_ref[...] = acc

    carry, acc = visit(page_refs[:pages], page_refs[pages:], None, carry_ref[...])
    carry_ref[...] = carry
    acc_ref[...] += acc

    @pl.when(g == pl.num_programs(1) - 1)
    def _():
        acc = acc_ref[...]
        lane_h = lax.broadcasted_iota(jnp.int32, (t_new, d_sb), 1) // HEAD_DIM
        out = jnp.zeros((t_new, d_sb), _F32)
        for hh in range(n_heads):
            out = jnp.where(lane_h == hh, acc[hh * t_new:(hh + 1) * t_new, :], out)
        o_ref[0] = out


def _sb_attend_sample(q, k_new, v_new, cache_k, cache_v, page_table, bias_rows, tri, batch, t_new):
    n, d_sb = q.shape
    n_heads = d_sb // HEAD_DIM
    n_pages = page_table.shape[1]
    page = cache_k.shape[1]
    pages = min(PAGES_PER_STEP, n_pages)
    assert page == LANES and n_pages % pages == 0 and t_new == SUBLANES
    groups = n_pages // pages
    ck = cache_k.transpose(0, 2, 3, 1)
    cv = cache_v.transpose(0, 2, 3, 1)
    as_page = lambda t: jnp.pad(t.reshape(batch, t_new, n_heads, HEAD_DIM).transpose(0, 2, 3, 1),
                                ((0, 0), (0, 0), (0, 0), (0, page - t_new)))
    q3, k3, v3 = q.reshape(batch, t_new, d_sb), as_page(k_new), as_page(v_new)
    tok = pl.BlockSpec((1, t_new, d_sb), lambda b, g, pt: (b, 0, 0))
    new_page = pl.BlockSpec((1, n_heads, HEAD_DIM, page), lambda b, g, pt: (b, 0, 0, 0))

    def page_spec(r):
        return pl.BlockSpec((1, n_heads, HEAD_DIM, page),
                            lambda b, g, pt: (pt[b, (groups - 1 - g) * pages + r], 0, 0, 0))

    out = pl.pallas_call(
        functools.partial(_sb_sample_kernel, t_new=t_new, n_heads=n_heads, pages=pages),
        out_shape=jax.ShapeDtypeStruct((batch, t_new, d_sb), _F32),
        grid_spec=pltpu.PrefetchScalarGridSpec(
            num_scalar_prefetch=1,
            grid=(batch, groups),
            in_specs=[tok, new_page, new_page,
                      pl.BlockSpec(bias_rows.shape, lambda b, g, pt: (0, 0)),
                      pl.BlockSpec(tri.shape, lambda b, g, pt: (0, 0))]
                     + [page_spec(r) for r in range(pages)] + [page_spec(r) for r in range(pages)],
            out_specs=tok,
            scratch_shapes=[pltpu.VMEM((n_heads * t_new, d_sb), _F32), pltpu.VMEM((n_heads * t_new, 1), _F32)],
        ),
        compiler_params=_params(("parallel", "arbitrary")),
        name="sb_sample",
    )(page_table, q3, k3, v3, bias_rows, tri, *([ck] * pages), *([cv] * pages))
    return out.reshape(n, d_sb)


def _seq_layout(tm, t_seq):
    return (tm // t_seq, t_seq) if tm >= t_seq else (1, tm)


def _boundary_rows(init_ref, carry_ref, tm, t_seq, depth):
    nseq, rows = _seq_layout(tm, t_seq)
    if nseq == 1:
        tiles_per_seq = t_seq // tm

        @pl.when(pl.program_id(0) % tiles_per_seq == 0)
        def _():
            carry_ref[...] = init_ref[0]

        c = carry_ref[...]
        return [jnp.broadcast_to(c[r:r + 1, :], (tm, c.shape[1])) for r in range(depth)]
    c = init_ref[...]
    d = c.shape[2]
    return [jnp.broadcast_to(c[:, r:r + 1, :], (nseq, rows, d)).reshape(tm, d) for r in range(depth)]


def _shifted(x, boundary, tm, t_seq, shift, depth):
    _, rows = _seq_layout(tm, t_seq)
    pos = lax.broadcasted_iota(jnp.int32, x.shape, 0) % rows
    out = pltpu.roll(x, shift, 0)
    for s in range(shift):
        out = jnp.where(pos == s, boundary[depth - (shift - s)], out)
    return out


def _init_spec(tm, t_seq, depth, d):
    nseq, _ = _seq_layout(tm, t_seq)
    if nseq == 1:
        tiles_per_seq = t_seq // tm
        return pl.BlockSpec((1, depth, d), lambda i: (i // tiles_per_seq, 0, 0))
    return pl.BlockSpec((nseq, depth, d), lambda i: (i, 0, 0))


def _rwkv_prep_kernel(p_ref, init_ref, mu_ref, w0_ref, a0_ref, w2_ref, a2_ref, g2_ref,
                      r_ref, k_ref, v_ref, w_ref, a_ref, g_ref, carry_ref, *, tm, t_seq, d_rw):
    p = p_ref[...]
    boundary = _boundary_rows(init_ref, carry_ref, tm, t_seq, 1)
    prev = _shifted(p, boundary, tm, t_seq, 1, 1)
    if _seq_layout(tm, t_seq)[0] == 1:
        carry_ref[...] = p[tm - 1:tm, :]
    xm = p + (prev - p) * mu_ref[...]
    r_ref[...] = xm[:, 0:d_rw]
    k_ref[...] = xm[:, d_rw:2 * d_rw]
    v_ref[...] = xm[:, 2 * d_rw:3 * d_rw]
    lora_wa = xm[:, 3 * d_rw:3 * d_rw + LORA_W + LORA_A]
    pg = xm[:, 3 * d_rw + LORA_W + LORA_A:]
    lw = _dot(jnp.tanh(lora_wa).astype(_BF16), w2_ref[...])
    la = _dot(lora_wa.astype(_BF16), a2_ref[...])
    log_w = -_softplus(-(w0_ref[...] + lw)) - 0.5
    w_ref[...] = jnp.exp(-jnp.exp(log_w))
    a_ref[...] = _sigmoid(a0_ref[...] + la)
    g_ref[...] = _dot(_sigmoid(pg).astype(_BF16), g2_ref[...])


def _time_major(tm, t_seq):
    return tm <= t_seq


def _time_major_spec(tm, t_seq, width):
    tiles_per_seq = t_seq // tm
    return pl.BlockSpec((tm, width), lambda i: (i % tiles_per_seq, i // tiles_per_seq))


def _rwkv_prep(prw, shift_init, mu, w0, a0, w2p, a2p, g2, t_seq):
    n, p_rw = prw.shape
    d_rw = w0.shape[1]
    tm = _token_tile(n, t_seq)
    row = lambda width: pl.BlockSpec((tm, width), lambda i: (i, 0))
    if _time_major(tm, t_seq):
        seq_shape, seq_spec = (t_seq, n // t_seq * d_rw), _time_major_spec(tm, t_seq, d_rw)
    else:
        seq_shape, seq_spec = (n, d_rw), row(d_rw)
    outs = tuple(jax.ShapeDtypeStruct(seq_shape, _F32) for _ in range(5)) + (jax.ShapeDtypeStruct((n, d_rw), _F32),)
    return pl.pallas_call(
        functools.partial(_rwkv_prep_kernel, tm=tm, t_seq=t_seq, d_rw=d_rw),
        out_shape=outs,
        grid=(n // tm,),
        in_specs=[row(p_rw), _init_spec(tm, t_seq, 1, p_rw)] + [_resident()] * 6,
        out_specs=(seq_spec,) * 5 + (row(d_rw),),
        scratch_shapes=[pltpu.VMEM((1, p_rw), _F32)],
        compiler_params=_params(("arbitrary",)),
        name="rwkv_prep",
    )(prw, shift_init, mu, w0, a0, w2p, a2p, g2)


def _rwkv_scan_kernel(r_ref, k_ref, v_ref, w_ref, a_ref, kk_tab, ka_tab, rk_tab, lnw_tab, lnb_tab, s0_ref,
                      y_ref, s_out_ref, s_ref, kk_ref, b_ref, k2_ref, *, steps):
    c = pl.program_id(1)
    n = HEAD_DIM

    @pl.when(c == 0)
    def _():
        s_ref[...] = s0_ref[...]

    def col_sum(x):
        return jnp.sum(x, axis=0, keepdims=True)

    def step(t, _):
        r_t, k_t, v_t, a_t = r_ref[t], k_ref[t], v_ref[t], a_ref[t]
        kk = k_t * kk_tab[...]
        kk = kk * lax.rsqrt(jnp.maximum(col_sum(kk * kk), KK_EPS))
        k2 = k_t * (1.0 + (a_t - 1.0) * ka_tab[...])
        kk_ref[...] = kk
        b_ref[...] = kk * a_t
        k2_ref[...] = k2
        bonus = col_sum(r_t * k2 * rk_tab[...]) * v_t

        def dot_kk(j, acc):
            return acc + s_ref[j] * kk_ref[pl.ds(j, 1), :]

        s_kk = lax.fori_loop(0, n, dot_kk, jnp.zeros((n, LANES), _F32), unroll=SCAN_UNROLL)

        def update(j, acc):
            row = pl.ds(j, 1)
            s_j = s_ref[j] * w_ref[t, row, :] - s_kk * b_ref[row, :] + v_t * k2_ref[row, :]
            s_ref[j] = s_j
            return acc + s_j * r_ref[t, row, :]

        y = lax.fori_loop(0, n, update, jnp.zeros((n, LANES), _F32), unroll=SCAN_UNROLL)
        mean = col_sum(y) * (1.0 / n)
        yc = y - mean
        var = col_sum(yc * yc) * (1.0 / n)
        y_ref[t] = yc * lax.rsqrt(var + GN_EPS) * lnw_tab[...] + lnb_tab[...] + bonus
        return 0

    lax.fori_loop(0, steps, step, 0)

    @pl.when(c == pl.num_programs(1) - 1)
    def _():
        s_out_ref[...] = s_ref[...]


def _rwkv_scan(r, k, v, w, a, tabs, s0):
    t_len, n, lanes = r.shape
    steps = min(RWKV_TIME_CHUNK, t_len)
    assert t_len % steps == 0 and lanes % LANES == 0 and n == HEAD_DIM
    seq = pl.BlockSpec((steps, n, LANES), lambda g, c: (c, 0, g))
    tab = pl.BlockSpec((n, LANES), lambda g, c: (0, g))
    st = pl.BlockSpec((n, n, LANES), lambda g, c: (0, 0, g))
    return pl.pallas_call(
        functools.partial(_rwkv_scan_kernel, steps=steps),
        out_shape=(jax.ShapeDtypeStruct((t_len, n, lanes), _F32), jax.ShapeDtypeStruct((n, n, lanes), _F32)),
        grid=(lanes // LANES, t_len // steps),
        in_specs=[seq] * 5 + [tab] * 5 + [st],
        out_specs=(seq, st),
        scratch_shapes=[pltpu.VMEM((n, n, LANES), _F32)] + [pltpu.VMEM((n, LANES), _F32)] * 3,
        compiler_params=_params(("parallel", "arbitrary")),
        name="rwkv_scan",
    )(r, k, v, w, a, *tabs, s0)


def _to_lanes(x, batch, t_len, lanes):
    if x.shape[0] == t_len and batch > 1:
        y = jnp.swapaxes(x.reshape(t_len, -1, HEAD_DIM), 1, 2)
    else:
        h = x.shape[1] // HEAD_DIM
        y = x.reshape(batch, t_len, h, HEAD_DIM).transpose(1, 3, 0, 2).reshape(t_len, HEAD_DIM, batch * h)
    return jnp.pad(y, ((0, 0), (0, 0), (0, lanes - y.shape[2])))


def _from_lanes(y, batch, heads, time_major):
    t_len = y.shape[0]
    y = y[:, :, :batch * heads]
    if time_major:
        return jnp.swapaxes(y, 1, 2).reshape(t_len, batch * heads * HEAD_DIM)
    return y.reshape(t_len, HEAD_DIM, batch, heads).transpose(2, 0, 3, 1).reshape(batch * t_len, heads * HEAD_DIM)


def _table(x, batch, lanes):
    h = x.size // HEAD_DIM
    t = jnp.tile(x.reshape(h, HEAD_DIM).T, (1, batch))
    return jnp.pad(t, ((0, 0), (0, lanes - batch * h)))


def _out_ffn_kernel(*refs, gated, d_half):
    if gated:
        x_ref, ma_ref, y_ref, g_ref, wo_ref, gain_ref, wg_ref, wu_ref, wd_ref, o_ref = refs
        mb = (y_ref[...] * g_ref[...]).astype(_BF16)
    else:
        x_ref, ma_ref, mb_ref, wo_ref, gain_ref, wg_ref, wu_ref, wd_ref, o_ref = refs
        mb = mb_ref[...].astype(_BF16)
    ma = ma_ref[...].astype(_BF16)
    x1 = x_ref[...] + _dot(ma, wo_ref[0:d_half, :]) + _dot(mb, wo_ref[d_half:, :])
    h = _rmsnorm_rows(x1, gain_ref[...]).astype(_BF16)
    gate = _dot(h, wg_ref[...])
    act = (gate * _sigmoid(gate) * _dot(h, wu_ref[...])).astype(_BF16)
    o_ref[...] = x1 + _dot(act, wd_ref[...])


def _out_ffn(x, mix, wo_bf, gain, wg_bf, wu_bf, wd_bf, t_seq):
    n, d = x.shape
    d_half = wo_bf.shape[0] // 2
    tm = _token_tile(n, t_seq)
    gated = len(mix) == 3
    if gated:
        row = pl.BlockSpec((tm, d_half), lambda i: (i, 0))
        y_spec = row if mix[1].shape == (n, d_half) else _time_major_spec(tm, t_seq, d_half)
        mix_specs = [row, y_spec, row]
    elif mix[0] is not mix[1]:
        mix_specs = [pl.BlockSpec((tm, d_half), lambda i: (i, 0))] * len(mix)
    else:
        mix_specs = [pl.BlockSpec((tm, d_half), lambda i: (i, 0)), pl.BlockSpec((tm, d_half), lambda i: (i, 1))]
    return pl.pallas_call(
        functools.partial(_out_ffn_kernel, gated=gated, d_half=d_half),
        out_shape=jax.ShapeDtypeStruct((n, d), _F32),
        grid=(n // tm,),
        in_specs=[pl.BlockSpec((tm, d), lambda i: (i, 0))] + mix_specs + [_resident()] * 5,
        out_specs=pl.BlockSpec((tm, d), lambda i: (i, 0)),
        compiler_params=_params(("parallel",)),
        name="out_ffn",
    )(x, *mix, wo_bf, gain, wg_bf, wu_bf, wd_bf)


def _conv_mix_kernel(x_ref, gain_ref, w_ref, cw_ref, init_ref, m_ref, state_ref, carry_ref, *, tm, t_seq, d_conv):
    h = _rmsnorm_rows(x_ref[...], gain_ref[...]).astype(_BF16)
    u = _dot(h, w_ref[:, d_conv:2 * d_conv]) * _dot(h, w_ref[:, 2 * d_conv:])
    depth = cw_ref.shape[0] - 1
    boundary = _boundary_rows(init_ref, carry_ref, tm, t_seq, depth)
    y = u * cw_ref[depth:depth + 1, :]
    for s in range(1, depth + 1):
        y = y + _shifted(u, boundary, tm, t_seq, s, depth) * cw_ref[depth - s:depth - s + 1, :]
    m_ref[...] = (_dot(h, w_ref[:, 0:d_conv]) * y).astype(m_ref.dtype)
    nseq, rows = _seq_layout(tm, t_seq)
    if nseq == 1:
        carry_ref[...] = u[tm - depth:, :]
        state_ref[0] = u[tm - depth:, :]
    else:
        state_ref[...] = u.reshape(nseq, rows, d_conv)[:, rows - depth:, :]


def _conv_mix(x, gain, w_bf, conv_w, init, t_seq):
    n, d = x.shape
    d_conv = conv_w.shape[1]
    depth = conv_w.shape[0] - 1
    tm = _token_tile(n, t_seq)
    return pl.pallas_call(
        functools.partial(_conv_mix_kernel, tm=tm, t_seq=t_seq, d_conv=d_conv),
        out_shape=(jax.ShapeDtypeStruct((n, d_conv), _BF16), jax.ShapeDtypeStruct(init.shape, _F32)),
        grid=(n // tm,),
        in_specs=[pl.BlockSpec((tm, d), lambda i: (i, 0)), _resident(), _resident(), _resident(),
                  _init_spec(tm, t_seq, depth, d_conv)],
        out_specs=(pl.BlockSpec((tm, d_conv), lambda i: (i, 0)), _init_spec(tm, t_seq, depth, d_conv)),
        scratch_shapes=[pltpu.VMEM((depth, d_conv), _F32)],
        compiler_params=_params(("arbitrary",)),
        name="conv_mix",
    )(x, gain, w_bf, conv_w, init)


def _strict_upper_sum_matrix(n):
    j = lax.broadcasted_iota(jnp.int32, (n, n), 0)
    s = lax.broadcasted_iota(jnp.int32, (n, n), 1)
    return (j > s).astype(_BF16)


def kernel(x_prompt, x_sample, cache_k, cache_v, state_wkv, state_shift, state_conv, page_table, norm_mix, norm_ffn, w_in_ab, q_norm, k_norm, sb_bias, mu_rw, w0, w2, a0, a2, g2, k_k, k_a, r_k, lnx_w, lnx_b, w_out_ab, w_in_c, conv_w, w_out_c, w_gate, w_up, w_down):
    bp, sp, d = x_prompt.shape
    bs, ts, _ = x_sample.shape
    depth = norm_mix.shape[0]
    d_rw = w0.shape[1]
    d_sb = w_out_ab.shape[1] - d_rw
    sb_heads, rw_heads = d_sb // HEAD_DIM, d_rw // HEAD_DIM
    p_rw = mu_rw.shape[1]
    groups = ((x_prompt.reshape(bp * sp, d), bp, sp), (x_sample.reshape(bs * ts, d), bs, ts))
    xs = [g[0] for g in groups]
    bf = lambda t: t.astype(_BF16)
    row = lambda t: t.reshape(1, -1)

    head_id = jnp.arange(d_sb) // HEAD_DIM
    seg = jnp.where(head_id[:, None] == head_id[None, :], 1.0 / HEAD_DIM, 0.0).astype(_BF16)
    tri_prompt = _strict_upper_sum_matrix(min(SUFFIX_CHUNK, ATTN_TILE, sp))
    tri_page = _strict_upper_sum_matrix(cache_k.shape[2])

    outs = {name: ([], []) for name in ("k", "v", "wkv", "shift", "conv")}
    for layer in range(depth):
        j = layer // 2
        if layer % 2 == 0:
            w_in = bf(w_in_ab[j])
            qg, kg = row(jnp.tile(q_norm[j], sb_heads)), row(jnp.tile(k_norm[j], sb_heads))
            w2p = bf(jnp.concatenate([w2[j], jnp.zeros((LORA_A, d_rw), _F32)], 0))
            a2p = bf(jnp.concatenate([jnp.zeros((LORA_W, d_rw), _F32), a2[j]], 0))
            bias_rows = jnp.broadcast_to(jnp.repeat(sb_bias[j], ts)[:, None], (sb_heads * ts, LANES))
            for gi, (_, batch, t_len) in enumerate(groups):
                x = xs[gi]
                q, k, v, prw = _in_proj(x, row(norm_mix[layer]), w_in, qg, kg, seg, t_len)
                if gi == 0:
                    o_sb = _sb_attend_prompt(q, k, v, sb_bias[j], tri_prompt, batch, t_len)
                    shift_init = jnp.zeros((batch, 1, p_rw), _F32)
                    wkv0 = jnp.zeros((batch, rw_heads, HEAD_DIM, HEAD_DIM), _F32)
                else:
                    o_sb = _sb_attend_sample(q, k, v, cache_k[j], cache_v[j], page_table, bias_rows, tri_page,
                                             batch, t_len)
                    shift_init = state_shift[j][:, None, :]
                    wkv0 = state_wkv[j]
                r_, k_, v_, w_, a_, g_ = _rwkv_prep(prw, shift_init, row(mu_rw[j]), row(w0[j]), row(a0[j]),
                                                    w2p, a2p, bf(g2[j]), t_len)
                lanes = -(-batch * rw_heads // LANES) * LANES
                seqs = [_to_lanes(t, batch, t_len, lanes) for t in (r_, k_, v_, w_, a_)]
                tabs = [_table(t, batch, lanes) for t in (k_k[j], k_a[j], r_k[j], lnx_w[j], lnx_b[j])]
                s0 = jnp.pad(wkv0.transpose(3, 2, 0, 1).reshape(HEAD_DIM, HEAD_DIM, batch * rw_heads),
                             ((0, 0), (0, 0), (0, lanes - batch * rw_heads)))
                y_l, s_l = _rwkv_scan(*seqs, tabs, s0)
                y_rw = _from_lanes(y_l, batch, rw_heads, time_major=r_.shape[0] != x.shape[0])
                wkv = s_l[:, :, :batch * rw_heads].reshape(HEAD_DIM, HEAD_DIM, batch, rw_heads).transpose(2, 3, 1, 0)
                xs[gi] = _out_ffn(x, (o_sb, y_rw, g_), bf(w_out_ab[j]), row(norm_ffn[layer]),
                                  bf(w_gate[layer]), bf(w_up[layer]), bf(w_down[layer]), t_len)
                outs["k"][gi].append(k.reshape(batch, t_len, sb_heads, HEAD_DIM))
                outs["v"][gi].append(v.reshape(batch, t_len, sb_heads, HEAD_DIM))
                outs["wkv"][gi].append(wkv)
                outs["shift"][gi].append(prw.reshape(batch, t_len, p_rw)[:, -1])
        else:
            for gi, (_, batch, t_len) in enumerate(groups):
                x = xs[gi]
                init = jnp.zeros((batch, conv_w.shape[1] - 1, conv_w.shape[2]), _F32) if gi == 0 else state_conv[j]
                m, buf = _conv_mix(x, row(norm_mix[layer]), bf(w_in_c[j]), conv_w[j], init, t_len)
                xs[gi] = _out_ffn(x, (m, m), bf(w_out_c[j]), row(norm_ffn[layer]),
                                  bf(w_gate[layer]), bf(w_up[layer]), bf(w_down[layer]), t_len)
                outs["conv"][gi].append(buf)
    stack = lambda name, gi: jnp.stack(outs[name][gi])
    return (xs[0].reshape(bp, sp, d), xs[1].reshape(bs, ts, d),
            stack("k", 0), stack("v", 0), stack("k", 1), stack("v", 1),
            stack("wkv", 0), stack("wkv", 1), stack("shift", 0), stack("shift", 1),
            stack("conv", 0), stack("conv", 1))
```

```python
import functools

import jax
import jax.numpy as jnp
from jax import lax
from jax.experimental import pallas as pl
from jax.experimental.pallas import tpu as pltpu

HEAD_DIM = 64
LORA_W = 64
LORA_A = 64
LORA_G = 128
RMS_EPS = 1e-6
GN_EPS = 64e-5
KK_EPS = 1e-24

LANES = 128
SUBLANES = 8
TOKEN_TILE = 512
ATTN_TILE = 512
SUFFIX_CHUNK = 256
ATTN_HEADS = 8
RWKV_TIME_CHUNK = 32
SCAN_UNROLL = 32
PAGES_PER_STEP = 32
PAGES_PER_DOT = 16
VMEM_LIMIT =56 * 1024 * 1024

_F32 = jnp.float32
_BF16 = jnp.bfloat16


def _params(semantics):
    return pltpu.CompilerParams(dimension_semantics=semantics, vmem_limit_bytes=VMEM_LIMIT)


def _resident():
    return pl.BlockSpec(memory_space=pltpu.VMEM)


def _dot(a, b):
    return jnp.dot(a, b, preferred_element_type=_F32)


def _dot_nt(a, b):
    return lax.dot_general(a, b, (((1,), (1,)), ((), ())), preferred_element_type=_F32)


def _softplus(x):
    return jnp.maximum(x, 0.0) + jnp.log(1.0 + jnp.exp(-jnp.abs(x)))


def _sigmoid(x):
    return 1.0 / (1.0 + jnp.exp(-x))


def _rmsnorm_rows(x, gain):
    return x * lax.rsqrt(jnp.mean(x * x, -1, keepdims=True) + RMS_EPS) * gain


def _split_hi_lo(x):
    hi = x.astype(_BF16)
    lo = (x - hi.astype(_F32)).astype(_BF16)
    return hi, lo


def _token_tile(n, t_seq):
    tm = min(TOKEN_TILE, n)
    assert n % tm == 0 and (t_seq % tm == 0 or tm % t_seq == 0), (n, tm, t_seq)
    return tm


def _in_proj_kernel(x_ref, gain_ref, w_ref, qg_ref, kg_ref, seg_ref, q_ref, k_ref, v_ref, p_ref, *t_refs, d_sb):
    h = _rmsnorm_rows(x_ref[...], gain_ref[...]).astype(_BF16)
    seg = seg_ref[...]

    def head_norm(p, g):
        return p * lax.rsqrt(_dot((p * p).astype(_BF16), seg) + RMS_EPS) * g

    q = head_norm(_dot(h, w_ref[:, 0:d_sb]), qg_ref[...])
    q_ref[...] = q * (HEAD_DIM ** -0.5)
    k = head_norm(_dot(h, w_ref[:, d_sb:2 * d_sb]), kg_ref[...])
    v = _dot(h, w_ref[:, 2 * d_sb:3 * d_sb])
    k_ref[...] = k
    v_ref[...] = v
    p_ref[...] = _dot(h, w_ref[:, 3 * d_sb:])
    if t_refs:
        t_refs[0][0] = k.T
        t_refs[1][0] = v.T


def _in_proj(x, gain, w_bf, qg, kg, seg, t_seq):
    n, d = x.shape
    p_ab = w_bf.shape[1]
    d_sb = qg.shape[1]
    p_rw = p_ab - 3 * d_sb
    tm = _token_tile(n, t_seq)
    row = lambda width: pl.BlockSpec((tm, width), lambda i: (i, 0))
    out_shape = [jax.ShapeDtypeStruct((n, d_sb), _F32)] * 3 + [jax.ShapeDtypeStruct((n, p_rw), _F32)]
    out_specs = [row(d_sb), row(d_sb), row(d_sb), row(p_rw)]
    if _time_major(tm, t_seq) and tm % LANES == 0:
        tiles_per_seq = t_seq // tm
        out_shape += [jax.ShapeDtypeStruct((n // t_seq, d_sb, t_seq), _F32)] * 2
        out_specs += [pl.BlockSpec((1, d_sb, tm), lambda i: (i // tiles_per_seq, 0, i % tiles_per_seq))] * 2
    return pl.pallas_call(
        functools.partial(_in_proj_kernel, d_sb=d_sb),
        out_shape=tuple(out_shape),
        grid=(n // tm,),
        in_specs=[row(d), _resident(), _resident(), _resident(), _resident(), _resident()],
        out_specs=tuple(out_specs),
        compiler_params=_params(("parallel",)),
        name="in_proj",
    )(x, gain, w_bf, qg, kg, seg)


def _suffix_sums(log_keep, tri):
    m = log_keep.shape[0]
    r = _dot(jnp.concatenate(_split_hi_lo(log_keep), 0), tri)
    return r[:m] + r[m:]


def _sb_block(neg_z, mask, carry, tri):
    log_keep = jnp.minimum(neg_z, 0.0) - jnp.log(1.0 + jnp.exp(-jnp.abs(neg_z)))
    log_sig = log_keep - neg_z
    if mask is not None:
        log_keep = jnp.where(mask, log_keep, 0.0)
    chunk = tri.shape[0]
    after = []
    for c0 in reversed(range(0, neg_z.shape[1], chunk)):
        part = log_keep[:, c0:c0 + chunk]
        after.append(_suffix_sums(part, tri) + carry)
        carry = carry + jnp.sum(part, -1, keepdims=True)
    w = jnp.exp(log_sig + jnp.concatenate(after[::-1], 1))
    if mask is not None:
        w = jnp.where(mask, w, 0.0)
    return w, carry


def _sb_prompt_kernel(bias_ref, q_ref, k_ref, v_ref, tri_ref, o_ref, *, tile, heads):
    group = pl.program_id(1)
    i = pl.program_id(2)
    tri = tri_ref[...]
    lane_head = lax.broadcasted_iota(jnp.int32, (tile, LANES), 1) // HEAD_DIM
    lanes_of = lambda hh: slice((hh // 2) * LANES, (hh // 2 + 1) * LANES)
    qh = []
    for hh in range(heads):
        q = (-q_ref[0, :, lanes_of(hh)]).astype(_BF16)
        qh.append(jnp.where(lane_head == hh % 2, q, jnp.zeros_like(q)))
    neg_bias = [-bias_ref[group * heads + hh] for hh in range(heads)]

    def visit(j, state, mask):
        start = pl.multiple_of(j * tile, tile)
        out = []
        for hh in range(heads):
            kb = k_ref[0, pl.ds(start, tile), lanes_of(hh)].astype(_BF16)
            vb = v_ref[0, pl.ds(start, tile), lanes_of(hh)].astype(_BF16)
            carry, acc = state[hh]
            w, carry = _sb_block(_dot_nt(qh[hh], kb) + neg_bias[hh], mask, carry, tri)
            out.append((carry, acc + _dot(w.astype(_BF16), vb)))
        return tuple(out)

    rows = lax.broadcasted_iota(jnp.int32, (tile, tile), 0)
    cols = lax.broadcasted_iota(jnp.int32, (tile, tile), 1)
    zero = (jnp.zeros((tile, 1), _F32), jnp.zeros((tile, LANES), _F32))
    state = visit(i, (zero,) * heads, cols < rows)
    state = lax.fori_loop(0, i, lambda n, s: visit(i - 1 - n, s, None), state)
    for p in range(heads // 2):
        o_ref[0, :, p * LANES:(p + 1) * LANES] = jnp.where(
            lane_head == 0, state[2 * p][1], state[2 * p + 1][1]).astype(o_ref.dtype)


def _sb_attend_prompt(q, k, v, bias, tri, batch, seq):
    n, d_sb = q.shape
    tile = min(ATTN_TILE, seq)
    width = ATTN_HEADS * HEAD_DIM
    assert seq % tile == 0 and d_sb % width == 0 and width % LANES == 0 and tile % tri.shape[0] == 0
    q3, k3, v3 = (t.reshape(batch, seq, d_sb) for t in (q, k, v))
    full = pl.BlockSpec((1, seq, width), lambda b, p, i: (b, 0, p))
    blk = pl.BlockSpec((1, tile, width), lambda b, p, i: (b, i, p))
    out = pl.pallas_call(
        functools.partial(_sb_prompt_kernel, tile=tile, heads=ATTN_HEADS),
        out_shape=jax.ShapeDtypeStruct((batch, seq, d_sb), _BF16),
        grid=(batch, d_sb // width, seq // tile),
        in_specs=[pl.BlockSpec(memory_space=pltpu.SMEM), blk, full, full, _resident()],
        out_specs=blk,
        compiler_params=_params(("parallel", "parallel", "arbitrary")),
        name="sb_prompt",
    )(bias, q3, k3, v3, tri)
    return out.reshape(n, d_sb)


def _sb_sample_kernel(pt_ref, q_ref, kn_ref, vn_ref, bias_ref, tri_ref, *rest, t_new, n_heads, pages):
    del pt_ref
    page_refs, (o_ref, acc_ref, carry_ref) = rest[:2 * pages], rest[2 * pages:]
    g = pl.program_id(1)
    rows_q = n_heads * t_new
    d_sb = n_heads * HEAD_DIM
    page = tri_ref.shape[0]
    tri = tri_ref[...]
    neg_bias = -bias_ref[...][:, 0:1]
    row_head = lax.broadcasted_iota(jnp.int32, (rows_q, d_sb), 0) // t_new
    lane_head = lax.broadcasted_iota(jnp.int32, (rows_q, d_sb), 1) // HEAD_DIM
    q_rows = jnp.concatenate([q_ref[0]] * n_heads, axis=0)
    qbd = jnp.where(row_head == lane_head, -q_rows, 0.0).astype(_BF16)

    def visit(k_refs, v_refs, mask, carry):
        side_by_side = lambda refs: jnp.concatenate([r[0].reshape(d_sb, page).astype(_BF16) for r in refs], 1)
        acc = None
        for r0 in reversed(range(0, len(k_refs), PAGES_PER_DOT)):
            neg_z = _dot(qbd, side_by_side(k_refs[r0:r0 + PAGES_PER_DOT])) + neg_bias
            w, carry = _sb_block(neg_z, mask, carry, tri)
            o = _dot_nt(w.astype(_BF16), side_by_side(v_refs[r0:r0 + PAGES_PER_DOT]))
            acc = o if acc is None else acc + o
        return carry, acc

    @pl.when(g == 0)
    def _():
        t_row = lax.broadcasted_iota(jnp.int32, (rows_q, page), 0) % t_new
        key = lax.broadcasted_iota(jnp.int32, (rows_q, page), 1)
        carry, acc = visit([kn_ref], [vn_ref], key < t_row, jnp.zeros((rows_q, 1), _F32))
        carry_ref[...] = carry
        acc_ref[...] = acc

    carry, acc = visit(page_refs[:pages], page_refs[pages:], None, carry_ref[...])
    carry_ref[...] = carry
    acc_ref[...] += acc

    @pl.when(g == pl.num_programs(1) - 1)
    def _():
        acc = acc_ref[...]
        lane_h = lax.broadcasted_iota(jnp.int32, (t_new, d_sb), 1) // HEAD_DIM
        out = jnp.zeros((t_new, d_sb), _F32)
        for hh in range(n_heads):
            out = jnp.where(lane_h == hh, acc[hh * t_new:(hh + 1) * t_new, :], out)
        o_ref[0] = out


def _sb_attend_sample(q, k_new, v_new, cache_k, cache_v, page_table, bias_rows, tri, batch, t_new):
    n, d_sb = q.shape
    n_heads = d_sb // HEAD_DIM
    n_pages = page_table.shape[1]
    page = cache_k.shape[1]
    pages = min(PAGES_PER_STEP, n_pages)
    assert page == LANES and n_pages % pages == 0 and t_new == SUBLANES
    groups = n_pages // pages
    ck = cache_k.transpose(0, 2, 3, 1)
    cv = cache_v.transpose(0, 2, 3, 1)
    as_page = lambda t: jnp.pad(t.reshape(batch, t_new, n_heads, HEAD_DIM).transpose(0, 2, 3, 1),
                                ((0, 0), (0, 0), (0, 0), (0, page - t_new)))
    q3, k3, v3 = q.reshape(batch, t_new, d_sb), as_page(k_new), as_page(v_new)
    tok = pl.BlockSpec((1, t_new, d_sb), lambda b, g, pt: (b, 0, 0))
    new_page = pl.BlockSpec((1, n_heads, HEAD_DIM, page), lambda b, g, pt: (b, 0, 0, 0))

    def page_spec(r):
        return pl.BlockSpec((1, n_heads, HEAD_DIM, page),
                            lambda b, g, pt: (pt[b, (groups - 1 - g) * pages + r], 0, 0, 0))

    out = pl.pallas_call(
        functools.partial(_sb_sample_kernel, t_new=t_new, n_heads=n_heads, pages=pages),
        out_shape=jax.ShapeDtypeStruct((batch, t_new, d_sb), _F32),
        grid_spec=pltpu.PrefetchScalarGridSpec(
            num_scalar_prefetch=1,
            grid=(batch, groups),
            in_specs=[tok, new_page, new_page,
                      pl.BlockSpec(bias_rows.shape, lambda b, g, pt: (0, 0)),
                      pl.BlockSpec(tri.shape, lambda b, g, pt: (0, 0))]
                     + [page_spec(r) for r in range(pages)] + [page_spec(r) for r in range(pages)],
            out_specs=tok,
            scratch_shapes=[pltpu.VMEM((n_heads * t_new, d_sb), _F32), pltpu.VMEM((n_heads * t_new, 1), _F32)],
        ),
        compiler_params=_params(("parallel", "arbitrary")),
        name="sb_sample",
    )(page_table, q3, k3, v3, bias_rows, tri, *([ck] * pages), *([cv] * pages))
    return out.reshape(n, d_sb)


def _seq_layout(tm, t_seq):
    return (tm // t_seq, t_seq) if tm >= t_seq else (1, tm)


def _boundary_rows(init_ref, carry_ref, tm, t_seq, depth):
    nseq, rows = _seq_layout(tm, t_seq)
    if nseq == 1:
        tiles_per_seq = t_seq // tm

        @pl.when(pl.program_id(0) % tiles_per_seq == 0)
        def _():
            carry_ref[...] = init_ref[0]

        c = carry_ref[...]
        return [jnp.broadcast_to(c[r:r + 1, :], (tm, c.shape[1])) for r in range(depth)]
    c = init_ref[...]
    d = c.shape[2]
    return [jnp.broadcast_to(c[:, r:r + 1, :], (nseq, rows, d)).reshape(tm, d) for r in range(depth)]


def _shifted(x, boundary, tm, t_seq, shift, depth):
    _, rows = _seq_layout(tm, t_seq)
    pos = lax.broadcasted_iota(jnp.int32, x.shape, 0) % rows
    out = pltpu.roll(x, shift, 0)
    for s in range(shift):
        out = jnp.where(pos == s, boundary[depth - (shift - s)], out)
    return out


def _init_spec(tm, t_seq, depth, d):
    nseq, _ = _seq_layout(tm, t_seq)
    if nseq == 1:
        tiles_per_seq = t_seq // tm
        return pl.BlockSpec((1, depth, d), lambda i: (i // tiles_per_seq, 0, 0))
    return pl.BlockSpec((nseq, depth, d), lambda i: (i, 0, 0))


def _rwkv_prep_kernel(p_ref, init_ref, mu_ref, w0_ref, a0_ref, w2_ref, a2_ref, g2_ref,
                      r_ref, k_ref, v_ref, w_ref, a_ref, g_ref, carry_ref, *, tm, t_seq, d_rw):
    p = p_ref[...]
    boundary = _boundary_rows(init_ref, carry_ref, tm, t_seq, 1)
    prev = _shifted(p, boundary, tm, t_seq, 1, 1)
    if _seq_layout(tm, t_seq)[0] == 1:
        carry_ref[...] = p[tm - 1:tm, :]
    xm = p + (prev - p) * mu_ref[...]
    r_ref[...] = xm[:, 0:d_rw]
    k_ref[...] = xm[:, d_rw:2 * d_rw]
    v_ref[...] = xm[:, 2 * d_rw:3 * d_rw]
    lora_wa = xm[:, 3 * d_rw:3 * d_rw + LORA_W + LORA_A]
    pg = xm[:, 3 * d_rw + LORA_W + LORA_A:]
    lw = _dot(jnp.tanh(lora_wa).astype(_BF16), w2_ref[...])
    la = _dot(lora_wa.astype(_BF16), a2_ref[...])
    log_w = -_softplus(-(w0_ref[...] + lw)) - 0.5
    w_ref[...] = jnp.exp(-jnp.exp(log_w))
    a_ref[...] = _sigmoid(a0_ref[...] + la)
    g_ref[...] = _dot(_sigmoid(pg).astype(_BF16), g2_ref[...])


def _time_major(tm, t_seq):
    return tm <= t_seq


def _time_major_spec(tm, t_seq, width):
    tiles_per_seq = t_seq // tm
    return pl.BlockSpec((tm, width), lambda i: (i % tiles_per_seq, i // tiles_per_seq))


def _rwkv_prep(prw, shift_init, mu, w0, a0, w2p, a2p, g2, t_seq):
    n, p_rw = prw.shape
    d_rw = w0.shape[1]
    tm = _token_tile(n, t_seq)
    row = lambda width: pl.BlockSpec((tm, width), lambda i: (i, 0))
    if _time_major(tm, t_seq):
        seq_shape, seq_spec = (t_seq, n // t_seq * d_rw), _time_major_spec(tm, t_seq, d_rw)
    else:
        seq_shape, seq_spec = (n, d_rw), row(d_rw)
    outs = tuple(jax.ShapeDtypeStruct(seq_shape, _F32) for _ in range(5)) + (jax.ShapeDtypeStruct((n, d_rw), _F32),)
    return pl.pallas_call(
        functools.partial(_rwkv_prep_kernel, tm=tm, t_seq=t_seq, d_rw=d_rw),
        out_shape=outs,
        grid=(n // tm,),
        in_specs=[row(p_rw), _init_spec(tm, t_seq, 1, p_rw)] + [_resident()] * 6,
        out_specs=(seq_spec,) * 5 + (row(d_rw),),
        scratch_shapes=[pltpu.VMEM((1, p_rw), _F32)],
        compiler_params=_params(("arbitrary",)),
        name="rwkv_prep",
    )(prw, shift_init, mu, w0, a0, w2p, a2p, g2)


def _rwkv_scan_kernel(r_ref, k_ref, v_ref, w_ref, a_ref, kk_tab, ka_tab, rk_tab, lnw_tab, lnb_tab, s0_ref,
                      y_ref, s_out_ref, s_ref, kk_ref, b_ref, k2_ref, *, steps):
    c = pl.program_id(1)
    n = HEAD_DIM

    @pl.when(c == 0)
    def _():
        s_ref[...] = s0_ref[...]

    def col_sum(x):
        return jnp.sum(x, axis=0, keepdims=True)

    def step(t, _):
        r_t, k_t, v_t, a_t = r_ref[t], k_ref[t], v_ref[t], a_ref[t]
        kk = k_t * kk_tab[...]
        kk = kk * lax.rsqrt(jnp.maximum(col_sum(kk * kk), KK_EPS))
        k2 = k_t * (1.0 + (a_t - 1.0) * ka_tab[...])
        kk_ref[...] = kk
        b_ref[...] = kk * a_t
        k2_ref[...] = k2
        bonus = col_sum(r_t * k2 * rk_tab[...]) * v_t

        def dot_kk(j, acc):
            return acc + s_ref[j] * kk_ref[pl.ds(j, 1), :]

        s_kk = lax.fori_loop(0, n, dot_kk, jnp.zeros((n, LANES), _F32), unroll=SCAN_UNROLL)

        def update(j, acc):
            row = pl.ds(j, 1)
            s_j = s_ref[j] * w_ref[t, row, :] - s_kk * b_ref[row, :] + v_t * k2_ref[row, :]
            s_ref[j] = s_j
            return acc + s_j * r_ref[t, row, :]

        y = lax.fori_loop(0, n, update, jnp.zeros((n, LANES), _F32), unroll=SCAN_UNROLL)
        mean = col_sum(y) * (1.0 / n)
        yc = y - mean
        var = col_sum(yc * yc) * (1.0 / n)
        y_ref[t] = yc * lax.rsqrt(var + GN_EPS) * lnw_tab[...] + lnb_tab[...] + bonus
        return 0

    lax.fori_loop(0, steps, step, 0)

    @pl.when(c == pl.num_programs(1) - 1)
    def _():
        s_out_ref[...] = s_ref[...]


def _rwkv_scan(r, k, v, w, a, tabs, s0):
    t_len, n, lanes = r.shape
    steps = min(RWKV_TIME_CHUNK, t_len)
    assert t_len % steps == 0 and lanes % LANES == 0 and n == HEAD_DIM
    seq = pl.BlockSpec((steps, n, LANES), lambda g, c: (c, 0, g))
    tab = pl.BlockSpec((n, LANES), lambda g, c: (0, g))
    st = pl.BlockSpec((n, n, LANES), lambda g, c: (0, 0, g))
    return pl.pallas_call(
        functools.partial(_rwkv_scan_kernel, steps=steps),
        out_shape=(jax.ShapeDtypeStruct((t_len, n, lanes), _F32), jax.ShapeDtypeStruct((n, n, lanes), _F32)),
        grid=(lanes // LANES, t_len // steps),
        in_specs=[seq] * 5 + [tab] * 5 + [st],
        out_specs=(seq, st),
        scratch_shapes=[pltpu.VMEM((n, n, LANES), _F32)] + [pltpu.VMEM((n, LANES), _F32)] * 3,
        compiler_params=_params(("parallel", "arbitrary")),
        name="rwkv_scan",
    )(r, k, v, w, a, *tabs, s0)


def _to_lanes(x, batch, t_len, lanes):
    if x.shape[0] == t_len and batch > 1:
        y = jnp.swapaxes(x.reshape(t_len, -1, HEAD_DIM), 1, 2)
    else:
        h = x.shape[1] // HEAD_DIM
        y = x.reshape(batch, t_len, h, HEAD_DIM).transpose(1, 3, 0, 2).reshape(t_len, HEAD_DIM, batch * h)
    return jnp.pad(y, ((0, 0), (0, 0), (0, lanes - y.shape[2])))


def _from_lanes(y, batch, heads, time_major):
    t_len = y.shape[0]
    y = y[:, :, :batch * heads]
    if time_major:
        return jnp.swapaxes(y, 1, 2).reshape(t_len, batch * heads * HEAD_DIM)
    return y.reshape(t_len, HEAD_DIM, batch, heads).transpose(2, 0, 3, 1).reshape(batch * t_len, heads * HEAD_DIM)


def _table(x, batch, lanes):
    h = x.size // HEAD_DIM
    t = jnp.tile(x.reshape(h, HEAD_DIM).T, (1, batch))
    return jnp.pad(t, ((0, 0), (0, lanes - batch * h)))


def _out_ffn_kernel(*refs, gated, d_half):
    if gated:
        x_ref, ma_ref, y_ref, g_ref, wo_ref, gain_ref, wg_ref, wu_ref, wd_ref, o_ref = refs
        mb = (y_ref[...] * g_ref[...]).astype(_BF16)
    else:
        x_ref, ma_ref, mb_ref, wo_ref, gain_ref, wg_ref, wu_ref, wd_ref, o_ref = refs
        mb = mb_ref[...].astype(_BF16)
    ma = ma_ref[...].astype(_BF16)
    x1 = x_ref[...] + _dot(ma, wo_ref[0:d_half, :]) + _dot(mb, wo_ref[d_half:, :])
    h = _rmsnorm_rows(x1, gain_ref[...]).astype(_BF16)
    gate = _dot(h, wg_ref[...])
    act = (gate * _sigmoid(gate) * _dot(h, wu_ref[...])).astype(_BF16)
    o_ref[...] = x1 + _dot(act, wd_ref[...])


def _out_ffn(x, mix, wo_bf, gain, wg_bf, wu_bf, wd_bf, t_seq):
    n, d = x.shape
    d_half = wo_bf.shape[0] // 2
    tm = _token_tile(n, t_seq)
    gated = len(mix) == 3
    if gated:
        row = pl.BlockSpec((tm, d_half), lambda i: (i, 0))
        y_spec = row if mix[1].shape == (n, d_half) else _time_major_spec(tm, t_seq, d_half)
        mix_specs = [row, y_spec, row]
    elif mix[0] is not mix[1]:
        mix_specs = [pl.BlockSpec((tm, d_half), lambda i: (i, 0))] * len(mix)
    else:
        mix_specs = [pl.BlockSpec((tm, d_half), lambda i: (i, 0)), pl.BlockSpec((tm, d_half), lambda i: (i, 1))]
    return pl.pallas_call(
        functools.partial(_out_ffn_kernel, gated=gated, d_half=d_half),
        out_shape=jax.ShapeDtypeStruct((n, d), _F32),
        grid=(n // tm,),
        in_specs=[pl.BlockSpec((tm, d), lambda i: (i, 0))] + mix_specs + [_resident()] * 5,
        out_specs=pl.BlockSpec((tm, d), lambda i: (i, 0)),
        compiler_params=_params(("parallel",)),
        name="out_ffn",
    )(x, *mix, wo_bf, gain, wg_bf, wu_bf, wd_bf)


def _conv_mix_kernel(x_ref, gain_ref, w_ref, cw_ref, init_ref, m_ref, state_ref, carry_ref, *, tm, t_seq, d_conv):
    h = _rmsnorm_rows(x_ref[...], gain_ref[...]).astype(_BF16)
    u = _dot(h, w_ref[:, d_conv:2 * d_conv]) * _dot(h, w_ref[:, 2 * d_conv:])
    depth = cw_ref.shape[0] - 1
    boundary = _boundary_rows(init_ref, carry_ref, tm, t_seq, depth)
    y = u * cw_ref[depth:depth + 1, :]
    for s in range(1, depth + 1):
        y = y + _shifted(u, boundary, tm, t_seq, s, depth) * cw_ref[depth - s:depth - s + 1, :]
    m_ref[...] = (_dot(h, w_ref[:, 0:d_conv]) * y).astype(m_ref.dtype)
    nseq, rows = _seq_layout(tm, t_seq)
    if nseq == 1:
        carry_ref[...] = u[tm - depth:, :]
        state_ref[0] = u[tm - depth:, :]
    else:
        state_ref[...] = u.reshape(nseq, rows, d_conv)[:, rows - depth:, :]


def _conv_mix(x, gain, w_bf, conv_w, init, t_seq):
    n, d = x.shape
    d_conv = conv_w.shape[1]
    depth = conv_w.shape[0] - 1
    tm = _token_tile(n, t_seq)
    return pl.pallas_call(
        functools.partial(_conv_mix_kernel, tm=tm, t_seq=t_seq, d_conv=d_conv),
        out_shape=(jax.ShapeDtypeStruct((n, d_conv), _BF16), jax.ShapeDtypeStruct(init.shape, _F32)),
        grid=(n // tm,),
        in_specs=[pl.BlockSpec((tm, d), lambda i: (i, 0)), _resident(), _resident(), _resident(),
                  _init_spec(tm, t_seq, depth, d_conv)],
        out_specs=(pl.BlockSpec((tm, d_conv), lambda i: (i, 0)), _init_spec(tm, t_seq, depth, d_conv)),
        scratch_shapes=[pltpu.VMEM((depth, d_conv), _F32)],
        compiler_params=_params(("arbitrary",)),
        name="conv_mix",
    )(x, gain, w_bf, conv_w, init)


def _strict_upper_sum_matrix(n):
    j = lax.broadcasted_iota(jnp.int32, (n, n), 0)
    s = lax.broadcasted_iota(jnp.int32, (n, n), 1)
    return (j > s).astype(_BF16)


def kernel(x_prompt, x_sample, cache_k, cache_v, state_wkv, state_shift, state_conv, page_table, norm_mix, norm_ffn, w_in_ab, q_norm, k_norm, sb_bias, mu_rw, w0, w2, a0, a2, g2, k_k, k_a, r_k, lnx_w, lnx_b, w_out_ab, w_in_c, conv_w, w_out_c, w_gate, w_up, w_down):
    bp, sp, d = x_prompt.shape
    bs, ts, _ = x_sample.shape
    depth = norm_mix.shape[0]
    d_rw = w0.shape[1]
    d_sb = w_out_ab.shape[1] - d_rw
    sb_heads, rw_heads = d_sb // HEAD_DIM, d_rw // HEAD_DIM
    p_rw = mu_rw.shape[1]
    groups = ((x_prompt.reshape(bp * sp, d), bp, sp), (x_sample.reshape(bs * ts, d), bs, ts))
    xs = [g[0] for g in groups]
    bf = lambda t: t.astype(_BF16)
    row = lambda t: t.reshape(1, -1)

    head_id = jnp.arange(d_sb) // HEAD_DIM
    seg = jnp.where(head_id[:, None] == head_id[None, :], 1.0 / HEAD_DIM, 0.0).astype(_BF16)
    tri_prompt = _strict_upper_sum_matrix(min(SUFFIX_CHUNK, ATTN_TILE, sp))
    tri_page = _strict_upper_sum_matrix(cache_k.shape[2])

    outs = {name: ([], []) for name in ("k", "v", "wkv", "shift", "conv")}
    for layer in range(depth):
        j = layer // 2
        if layer % 2 == 0:
            w_in = bf(w_in_ab[j])
            qg, kg = row(jnp.tile(q_norm[j], sb_heads)), row(jnp.tile(k_norm[j], sb_heads))
            w2p = bf(jnp.concatenate([w2[j], jnp.zeros((LORA_A, d_rw), _F32)], 0))
            a2p = bf(jnp.concatenate([jnp.zeros((LORA_W, d_rw), _F32), a2[j]], 0))
            bias_rows = jnp.broadcast_to(jnp.repeat(sb_bias[j], ts)[:, None], (sb_heads * ts, LANES))
            for gi, (_, batch, t_len) in enumerate(groups):
                x = xs[gi]
                q, k, v, prw, *kv_t = _in_proj(x, row(norm_mix[layer]), w_in, qg, kg, seg, t_len)
                if gi == 0:
                    o_sb = _sb_attend_prompt(q, k, v, sb_bias[j], tri_prompt, batch, t_len)
                    shift_init = jnp.zeros((batch, 1, p_rw), _F32)
                    wkv0 = jnp.zeros((batch, rw_heads, HEAD_DIM, HEAD_DIM), _F32)
                else:
                    o_sb = _sb_attend_sample(q, k, v, cache_k[j], cache_v[j], page_table, bias_rows, tri_page,
                                             batch, t_len)
                    shift_init = state_shift[j][:, None, :]
                    wkv0 = state_wkv[j]
                r_, k_, v_, w_, a_, g_ = _rwkv_prep(prw, shift_init, row(mu_rw[j]), row(w0[j]), row(a0[j]),
                                                    w2p, a2p, bf(g2[j]), t_len)
                lanes = -(-batch * rw_heads // LANES) * LANES
                seqs = [_to_lanes(t, batch, t_len, lanes) for t in (r_, k_, v_, w_, a_)]
                tabs = [_table(t, batch, lanes) for t in (k_k[j], k_a[j], r_k[j], lnx_w[j], lnx_b[j])]
                s0 = jnp.pad(wkv0.transpose(3, 2, 0, 1).reshape(HEAD_DIM, HEAD_DIM, batch * rw_heads),
                             ((0, 0), (0, 0), (0, lanes - batch * rw_heads)))
                y_l, s_l = _rwkv_scan(*seqs, tabs, s0)
                y_rw = _from_lanes(y_l, batch, rw_heads, time_major=r_.shape[0] != x.shape[0])
                wkv = s_l[:, :, :batch * rw_heads].reshape(HEAD_DIM, HEAD_DIM, batch, rw_heads).transpose(2, 3, 1, 0)
                xs[gi] = _out_ffn(x, (o_sb, y_rw, g_), bf(w_out_ab[j]), row(norm_ffn[layer]),
                                  bf(w_gate[layer]), bf(w_up[layer]), bf(w_down[layer]), t_len)
                for name, flat, *t in (("k", k, *kv_t[:1]), ("v", v, *kv_t[1:])):
                    outs[name][gi].append(t[0].reshape(batch, sb_heads, HEAD_DIM, t_len).transpose(0, 3, 1, 2) if t
                                          else flat.reshape(batch, t_len, sb_heads, HEAD_DIM))
                outs["wkv"][gi].append(wkv)
                outs["shift"][gi].append(prw.reshape(batch, t_len, p_rw)[:, -1])
        else:
            for gi, (_, batch, t_len) in enumerate(groups):
                x = xs[gi]
                init = jnp.zeros((batch, conv_w.shape[1] - 1, conv_w.shape[2]), _F32) if gi == 0 else state_conv[j]
                m, buf = _conv_mix(x, row(norm_mix[layer]), bf(w_in_c[j]), conv_w[j], init, t_len)
                xs[gi] = _out_ffn(x, (m, m), bf(w_out_c[j]), row(norm_ffn[layer]),
                                  bf(w_gate[layer]), bf(w_up[layer]), bf(w_down[layer]), t_len)
                outs["conv"][gi].append(buf)
    stack = lambda name, gi: jnp.stack(outs[name][gi])
    return (xs[0].reshape(bp, sp, d), xs[1].reshape(bs, ts, d),
            stack("k", 0), stack("v", 0), stack("k", 1), stack("v", 1),
            stack("wkv", 0), stack("wkv", 1), stack("shift", 0), stack("shift", 1),
            stack("conv", 0), stack("conv", 1))
```
